```python
import math
import jax, jax.numpy as jnp
from jax import lax
import numpy as np

D_MODEL = 2048
BATCH = 1
SEQ = 8192
DEPTH = 4

CHUNK = 64
N_MIXERS = 2
N_GDN_LAYERS = (DEPTH + 1) // 2
N_S5_LAYERS = DEPTH // 2
GDN_K_HEADS = 16
GDN_V_HEADS = 32
GDN_HEAD_K = 128
GDN_HEAD_V = 128
GDN_CONV = 4
GDN_KEY_DIM = GDN_K_HEADS * GDN_HEAD_K
GDN_VAL_DIM = GDN_V_HEADS * GDN_HEAD_V
GDN_CONV_DIM = 2 * GDN_KEY_DIM + GDN_VAL_DIM
GDN_PROJ_DIM = GDN_CONV_DIM + GDN_VAL_DIM + 2 * GDN_V_HEADS
S5_GROUP_CH = 16
S5_GROUPS = D_MODEL // S5_GROUP_CH
S5_STATE = 64
FF_DIM = 4 * D_MODEL
PLE_DIM = 256
NORM_EPS = 1e-6
L2_EPS = 1e-6

kernel_name = 'hybrid_gdn_s5_stream_encoder'


def rms_norm(x, g):
    xf = x.astype(jnp.float32)
    y = xf * lax.rsqrt(jnp.mean(xf * xf, axis=-1, keepdims=True) + NORM_EPS)
    return (y * g.astype(jnp.float32)).astype(x.dtype)


def l2_normalize(x):
    xf = x.astype(jnp.float32)
    return xf * lax.rsqrt(jnp.sum(xf * xf, axis=-1, keepdims=True) + L2_EPS)


def causal_depthwise_conv(x, w):
    k, c = w.shape
    return lax.conv_general_dilated(
        x, w[:, None, :].astype(x.dtype), window_strides=(1,), padding=[(k - 1, 0)],
        dimension_numbers=('NWC', 'WIO', 'NWC'), feature_group_count=c)


def gated_delta_rule_chunked(q, k, v, g, beta):
    bsz, seq, nh, dk = q.shape
    dv = v.shape[-1]
    nc = seq // CHUNK

    def to_chunks(t):
        return t.reshape(bsz, nc, CHUNK, nh, -1).transpose(0, 3, 1, 2, 4)

    q, k, v = to_chunks(q), to_chunks(k), to_chunks(v)
    g = to_chunks(g[..., None])[..., 0]
    beta = to_chunks(beta[..., None])[..., 0]
    gc = jnp.cumsum(g, axis=-1)
    idx = jnp.arange(CHUNK)
    causal = idx[:, None] >= idx[None, :]
    strict = idx[:, None] > idx[None, :]
    decay = jnp.where(causal, jnp.exp(jnp.minimum(gc[..., :, None] - gc[..., None, :], 0.0)), 0.0)
    kb = k * beta[..., None]
    m = jnp.where(strict, jnp.einsum('bhnid,bhnjd->bhnij', kb, k) * decay, 0.0)
    eye = jnp.eye(CHUNK, dtype=m.dtype)
    t_inv = lax.linalg.triangular_solve(eye + m, jnp.broadcast_to(eye, m.shape),
                                        left_side=True, lower=True, unit_diagonal=True)
    u = jnp.einsum('bhnij,bhnjd->bhnid', t_inv, v * beta[..., None])
    w = jnp.einsum('bhnij,bhnjd->bhnid', t_inv, kb * jnp.exp(gc)[..., None])
    qk = jnp.where(causal, jnp.einsum('bhnid,bhnjd->bhnij', q, k) * decay, 0.0)
    g_last = gc[..., -1]
    qg = q * jnp.exp(gc)[..., None]
    kd = k * jnp.exp(g_last[..., None] - gc)[..., None]

    def step(state, xs):
        qg_n, kd_n, w_n, u_n, qk_n, gl_n = xs
        v_new = u_n - jnp.einsum('bhck,bhkv->bhcv', w_n, state)
        o_n = (jnp.einsum('bhck,bhkv->bhcv', qg_n, state)
               + jnp.einsum('bhij,bhjv->bhiv', qk_n, v_new))
        state = state * jnp.exp(gl_n)[..., None, None] + jnp.einsum('bhck,bhcv->bhkv', kd_n, v_new)
        return state, o_n

    s0 = jnp.zeros((bsz, nh, dk, dv), jnp.float32)
    xs = tuple(jnp.moveaxis(t, 2, 0) for t in (qg, kd, w, u, qk, g_last))
    _, o = lax.scan(step, s0, xs)
    return o.transpose(1, 0, 3, 2, 4).reshape(bsz, seq, nh, dv)


def gdn_mixer(hn, w_in, conv_w, a_log, dt_bias, o_norm, w_out):
    bsz, seq, _ = hn.shape
    proj = hn @ w_in
    qkv = proj[..., :GDN_CONV_DIM]
    z = proj[..., GDN_CONV_DIM:GDN_CONV_DIM + GDN_VAL_DIM]
    b = proj[..., GDN_CONV_DIM + GDN_VAL_DIM:GDN_CONV_DIM + GDN_VAL_DIM + GDN_V_HEADS]
    a = proj[..., GDN_CONV_DIM + GDN_VAL_DIM + GDN_V_HEADS:]
    qkv = jax.nn.silu(causal_depthwise_conv(qkv, conv_w))
    q = qkv[..., :GDN_KEY_DIM].reshape(bsz, seq, GDN_K_HEADS, GDN_HEAD_K)
    k = qkv[..., GDN_KEY_DIM:2 * GDN_KEY_DIM].reshape(bsz, seq, GDN_K_HEADS, GDN_HEAD_K)
    v = qkv[..., 2 * GDN_KEY_DIM:].reshape(bsz, seq, GDN_V_HEADS, GDN_HEAD_V).astype(jnp.float32)
    rep = GDN_V_HEADS // GDN_K_HEADS
    q = jnp.repeat(l2_normalize(q), rep, axis=2) * (GDN_HEAD_K ** -0.5)
    k = jnp.repeat(l2_normalize(k), rep, axis=2)
    beta = jax.nn.sigmoid(b.astype(jnp.float32))
    g = -jnp.exp(a_log.astype(jnp.float32)) * jax.nn.softplus(a.astype(jnp.float32) + dt_bias.astype(jnp.float32))
    o = gated_delta_rule_chunked(q, k, v, g, beta)
    o = rms_norm(o, o_norm) * jax.nn.silu(z.reshape(bsz, seq, GDN_V_HEADS, GDN_HEAD_V).astype(jnp.float32))
    return o.reshape(bsz, seq, GDN_VAL_DIM).astype(hn.dtype) @ w_out


def s5_mixer(hn, w_in, lam_re, lam_im, log_step, b_re, b_im, c_re, c_im, d_skip, w_out):
    bsz, seq, _ = hn.shape
    u = (hn @ w_in).astype(jnp.float32)
    ug = u.reshape(bsz, seq, S5_GROUPS, S5_GROUP_CH).astype(jnp.complex64)
    lam = lax.complex(lam_re.astype(jnp.float32), lam_im.astype(jnp.float32))
    step = jnp.exp(log_step.astype(jnp.float32))[:, None]
    lam_bar = jnp.exp(lam * step)
    b_bar = lax.complex(b_re.astype(jnp.float32), b_im.astype(jnp.float32)) * ((lam_bar - 1.0) / lam)[..., None]
    bu = jnp.einsum('bsgc,gpc->bsgp', ug, b_bar)
    a = jnp.broadcast_to(lam_bar, bu.shape)

    def combine(left, right):
        a1, x1 = left
        a2, x2 = right
        return a1 * a2, a2 * x1 + x2

    _, states = lax.associative_scan(combine, (a, bu), axis=1)
    c = lax.complex(c_re.astype(jnp.float32), c_im.astype(jnp.float32))
    y = jnp.einsum('bsgp,gcp->bsgc', states, c).real.reshape(bsz, seq, D_MODEL)
    y = y + d_skip.astype(jnp.float32) * u
    h = jax.nn.gelu(y).astype(hn.dtype)
    vg = h @ w_out
    return vg[..., :D_MODEL] * jax.nn.sigmoid(vg[..., D_MODEL:])


def squared_relu_mlp(hn, w_up, w_down):
    return jnp.square(jax.nn.relu(hn @ w_up)) @ w_down


def setup_inputs(seed: int = 0) -> dict:
    key = jax.random.key(seed)
    ks = iter(jax.random.split(key, 40))

    def normal(shape, scale):
        return jax.random.normal(next(ks), shape, jnp.float32) * scale

    def gain(shape):
        return 1.0 + normal(shape, 0.02)

    x = normal((BATCH, SEQ, D_MODEL), 1.0)
    p = normal((DEPTH, BATCH, SEQ, PLE_DIM), 1.0)
    norm_mix = gain((DEPTH, D_MODEL))
    norm_mlp = gain((DEPTH, D_MODEL))
    norm_ple = gain((DEPTH, D_MODEL))
    norm_final = gain((D_MODEL,))
    gdn_w_in = normal((N_GDN_LAYERS, D_MODEL, GDN_PROJ_DIM), D_MODEL ** -0.5)
    gdn_conv_w = normal((N_GDN_LAYERS, GDN_CONV, GDN_CONV_DIM), GDN_CONV ** -0.5)
    gdn_a_log = jnp.log(jax.random.uniform(next(ks), (N_GDN_LAYERS, GDN_V_HEADS), jnp.float32, 1.0, 16.0))
    dt = jnp.exp(jax.random.uniform(next(ks), (N_GDN_LAYERS, GDN_V_HEADS), jnp.float32,
                                    math.log(1e-3), math.log(1e-1)))
    gdn_dt_bias = dt + jnp.log(-jnp.expm1(-dt))
    gdn_o_norm = gain((N_GDN_LAYERS, GDN_HEAD_V))
    gdn_w_out = normal((N_GDN_LAYERS, GDN_VAL_DIM, D_MODEL), GDN_VAL_DIM ** -0.5)
    s5_w_in = normal((N_S5_LAYERS, D_MODEL, D_MODEL), D_MODEL ** -0.5)
    n_idx = jnp.arange(S5_STATE, dtype=jnp.float32)
    s5_lam_re = -0.5 + normal((N_S5_LAYERS, S5_GROUPS, S5_STATE), 0.01)
    s5_lam_im = math.pi * n_idx + normal((N_S5_LAYERS, S5_GROUPS, S5_STATE), 0.01)
    s5_log_step = jax.random.uniform(next(ks), (N_S5_LAYERS, S5_GROUPS), jnp.float32,
                                     math.log(1e-3), math.log(1e-1))
    s5_b_re = normal((N_S5_LAYERS, S5_GROUPS, S5_STATE, S5_GROUP_CH), (2 * S5_GROUP_CH) ** -0.5)
    s5_b_im = normal((N_S5_LAYERS, S5_GROUPS, S5_STATE, S5_GROUP_CH), (2 * S5_GROUP_CH) ** -0.5)
    s5_c_re = normal((N_S5_LAYERS, S5_GROUPS, S5_GROUP_CH, S5_STATE), S5_STATE ** -0.5)
    s5_c_im = normal((N_S5_LAYERS, S5_GROUPS, S5_GROUP_CH, S5_STATE), S5_STATE ** -0.5)
    s5_d = normal((N_S5_LAYERS, D_MODEL), 1.0)
    s5_w_out = normal((N_S5_LAYERS, D_MODEL, 2 * D_MODEL), D_MODEL ** -0.5)
    mlp_w_up = normal((DEPTH, D_MODEL, FF_DIM), D_MODEL ** -0.5)
    mlp_w_down = normal((DEPTH, FF_DIM, D_MODEL), FF_DIM ** -0.5)
    ple_w_proj = normal((DEPTH, PLE_DIM, D_MODEL), PLE_DIM ** -0.5)
    ple_w_gate = normal((DEPTH, D_MODEL, D_MODEL), D_MODEL ** -0.5)
    return {'x': x, 'p': p, 'norm_mix': norm_mix, 'norm_mlp': norm_mlp, 'norm_ple': norm_ple,
            'norm_final': norm_final, 'gdn_w_in': gdn_w_in, 'gdn_conv_w': gdn_conv_w,
            'gdn_a_log': gdn_a_log, 'gdn_dt_bias': gdn_dt_bias, 'gdn_o_norm': gdn_o_norm,
            'gdn_w_out': gdn_w_out, 's5_w_in': s5_w_in, 's5_lam_re': s5_lam_re, 's5_lam_im': s5_lam_im,
            's5_log_step': s5_log_step, 's5_b_re': s5_b_re, 's5_b_im': s5_b_im, 's5_c_re': s5_c_re,
            's5_c_im': s5_c_im, 's5_d': s5_d, 's5_w_out': s5_w_out, 'mlp_w_up': mlp_w_up,
            'mlp_w_down': mlp_w_down, 'ple_w_proj': ple_w_proj, 'ple_w_gate': ple_w_gate}


def reference(x, p, norm_mix, norm_mlp, norm_ple, norm_final, gdn_w_in, gdn_conv_w, gdn_a_log,
              gdn_dt_bias, gdn_o_norm, gdn_w_out, s5_w_in, s5_lam_re, s5_lam_im, s5_log_step,
              s5_b_re, s5_b_im, s5_c_re, s5_c_im, s5_d, s5_w_out, mlp_w_up, mlp_w_down,
              ple_w_proj, ple_w_gate):
    h = x
    for i in range(DEPTH):
        j = i // N_MIXERS
        hn = rms_norm(h, norm_mix[i])
        if i % N_MIXERS == 0:
            mix = gdn_mixer(hn, gdn_w_in[j], gdn_conv_w[j], gdn_a_log[j], gdn_dt_bias[j],
                            gdn_o_norm[j], gdn_w_out[j])
        else:
            mix = s5_mixer(hn, s5_w_in[j], s5_lam_re[j], s5_lam_im[j], s5_log_step[j],
                           s5_b_re[j], s5_b_im[j], s5_c_re[j], s5_c_im[j], s5_d[j], s5_w_out[j])
        h = h + mix.astype(h.dtype)
        h = h + squared_relu_mlp(rms_norm(h, norm_mlp[i]), mlp_w_up[i], mlp_w_down[i]).astype(h.dtype)
        gate = jax.nn.sigmoid((rms_norm(h, norm_ple[i]) @ ple_w_gate[i]).astype(jnp.float32))
        h = h + (gate * (p[i] @ ple_w_proj[i]).astype(jnp.float32)).astype(h.dtype)
    return rms_norm(h, norm_final)
```

```python
import functools
import math

import jax
import jax.numpy as jnp
from jax import lax
from jax.experimental import pallas as pl
from jax.experimental.pallas import tpu as pltpu

F32 = jnp.float32
BF16 = jnp.bfloat16

NORM_EPS = 1e-6
L2_EPS = 1e-6

GDN_K_HEADS = 16
GDN_V_HEADS = 32
GDN_HEAD = 128
GDN_CONV = 4
GDN_KEY_DIM = GDN_K_HEADS * GDN_HEAD
GDN_VAL_DIM = GDN_V_HEADS * GDN_HEAD
GDN_CONV_DIM = 2 * GDN_KEY_DIM + GDN_VAL_DIM
GDN_CHUNK = 64
GATE_LANES = 128

S5_GROUP_CH = 16
S5_STATE = 64
S5_BLOCK = 16
S5_BLOCK_W = S5_BLOCK * S5_GROUP_CH

SUBLANES = 8
VMEM_LIMIT_BYTES = 56 * 1024 * 1024


def _cparams(n_axes):
    return pltpu.CompilerParams(
        dimension_semantics=("arbitrary",) * n_axes,
        vmem_limit_bytes=VMEM_LIMIT_BYTES)


def _rms_rows(x, gain):
    ms = jnp.mean(x * x, axis=-1, keepdims=True)
    return x * lax.rsqrt(ms + NORM_EPS) * gain


def _sigmoid(x):
    return 1.0 / (1.0 + jnp.exp(-x))


def _softplus(x):
    return jnp.maximum(x, 0.0) + jnp.log(1.0 + jnp.exp(-jnp.abs(x)))


def _gelu_tanh(x):
    c = math.sqrt(2.0 / math.pi)
    return 0.5 * x * (1.0 + jnp.tanh(c * (x + 0.044715 * (x * x * x))))


def _dot(a, b):
    return jnp.dot(a, b, preferred_element_type=F32)


def _gdn_qkv_kernel(h_ref, g_ref, w_ref, cw_ref, o_ref, hn_scr, tail_scr, buf_scr,
                    *, tm, tn, n_qk_blocks):
    i = pl.program_id(0)
    j = pl.program_id(1)

    @pl.when(j == 0)
    def _():
        hn_scr[...] = _rms_rows(h_ref[...], g_ref[...]).astype(BF16)

    @pl.when(i == 0)
    def _():
        tail_scr[j] = jnp.zeros((SUBLANES, tn), F32)

    acc = _dot(hn_scr[...], w_ref[...])
    buf_scr[0:SUBLANES, :] = tail_scr[j]
    buf_scr[SUBLANES:, :] = acc
    tail_scr[j] = acc[tm - SUBLANES:, :]
    cw = cw_ref[...]
    y = acc * cw[3:4, :]
    for tap in range(GDN_CONV - 1):
        shift = GDN_CONV - 1 - tap
        y = y + buf_scr[pl.ds(SUBLANES - shift, tm), :] * cw[tap:tap + 1, :]
    y = y * _sigmoid(y)

    @pl.when(j >= n_qk_blocks)
    def _():
        o_ref[...] = y

    @pl.when(j < n_qk_blocks)
    def _():
        scale = jnp.where(j < n_qk_blocks // 2, GDN_HEAD ** -0.5, 1.0).astype(F32)
        for hh in range(tn // GDN_HEAD):
            sl = slice(hh * GDN_HEAD, (hh + 1) * GDN_HEAD)
            xs = y[:, sl]
            inv = lax.rsqrt(jnp.sum(xs * xs, axis=-1, keepdims=True) + L2_EPS)
            o_ref[:, sl] = xs * (inv * scale)


def _gdn_qkv(h, gain, w_qkv, conv_w, *, tm, tn):
    s, d = h.shape
    n = w_qkv.shape[1]
    kern = functools.partial(_gdn_qkv_kernel, tm=tm, tn=tn,
                             n_qk_blocks=2 * GDN_KEY_DIM // tn)
    return pl.pallas_call(
        kern,
        grid=(s // tm, n // tn),
        in_specs=[
            pl.BlockSpec((tm, d), lambda i, j: (i, 0)),
            pl.BlockSpec((1, d), lambda i, j: (0, 0)),
            pl.BlockSpec((d, tn), lambda i, j: (0, j)),
            pl.BlockSpec((GDN_CONV, tn), lambda i, j: (0, j)),
        ],
        out_specs=pl.BlockSpec((tm, tn), lambda i, j: (i, j)),
        out_shape=jax.ShapeDtypeStruct((s, n), F32),
        scratch_shapes=[
            pltpu.VMEM((tm, d), BF16),
            pltpu.VMEM((n // tn, SUBLANES, tn), F32),
            pltpu.VMEM((tm + SUBLANES, tn), F32),
        ],
        compiler_params=_cparams(2),
        name="gdn_qkv",
    )(h, gain, w_qkv, conv_w)


def _gdn_zgate_kernel(h_ref, g_ref, wz_ref, wba_ref, alog_ref, dtb_ref, tri_ref,
                      z_ref, gate_ref, hn_scr, *, tm):
    j = pl.program_id(1)

    @pl.when(j == 0)
    def _():
        hn = _rms_rows(h_ref[...], g_ref[...]).astype(BF16)
        hn_scr[...] = hn
        ba = _dot(hn, wba_ref[...])
        beta = _sigmoid(ba)
        g = -jnp.exp(alog_ref[...]) * _softplus(ba + dtb_ref[...])
        lane = lax.broadcasted_iota(jnp.int32, (GDN_CHUNK, GATE_LANES), 1)
        tri = tri_ref[...]
        for c in range(tm // GDN_CHUNK):
            rows = slice(c * GDN_CHUNK, (c + 1) * GDN_CHUNK)
            gc = jnp.dot(tri, g[rows, :], preferred_element_type=F32,
                         precision=lax.Precision.HIGHEST)
            gate_ref[rows, :] = jnp.where(lane < GDN_V_HEADS, beta[rows, :], gc)

    z = _dot(hn_scr[...], wz_ref[...])
    z_ref[...] = z * _sigmoid(z)


def _gdn_zgate(h, gain, w_z, w_ba, alog_row, dtb_row, *, tm, tn):
    s, d = h.shape
    n = w_z.shape[1]
    tri = jnp.tril(jnp.ones((GDN_CHUNK, GDN_CHUNK), F32))
    return pl.pallas_call(
        functools.partial(_gdn_zgate_kernel, tm=tm),
        grid=(s // tm, n // tn),
        in_specs=[
            pl.BlockSpec((tm, d), lambda i, j: (i, 0)),
            pl.BlockSpec((1, d), lambda i, j: (0, 0)),
            pl.BlockSpec((d, tn), lambda i, j: (0, j)),
            pl.BlockSpec((d, GATE_LANES), lambda i, j: (0, 0)),
            pl.BlockSpec((1, GATE_LANES), lambda i, j: (0, 0)),
            pl.BlockSpec((1, GATE_LANES), lambda i, j: (0, 0)),
            pl.BlockSpec((GDN_CHUNK, GDN_CHUNK), lambda i, j: (0, 0)),
        ],
        out_specs=[
            pl.BlockSpec((tm, tn), lambda i, j: (i, j)),
            pl.BlockSpec((tm, GATE_LANES), lambda i, j: (i, 0)),
        ],
        out_shape=[
            jax.ShapeDtypeStruct((s, n), F32),
            jax.ShapeDtypeStruct((s, GATE_LANES), F32),
        ],
        scratch_shapes=[pltpu.VMEM((tm, d), BF16)],
        compiler_params=_cparams(2),
        name="gdn_zgate",
    )(h, gain, w_z, w_ba, alog_row, dtb_row, tri)


def _gdn_core_kernel(q_ref, k_ref, v_ref, z_ref, gc_ref, gr_ref, on_ref, o_ref, s_scr,
                     *, tb):
    kh = pl.program_id(0)
    t = pl.program_id(1)
    c_sz = GDN_CHUNK

    @pl.when(t == 0)
    def _():
        s_scr[...] = jnp.zeros_like(s_scr)

    gates = gc_ref[...]
    lane = lax.broadcasted_iota(jnp.int32, (tb, GATE_LANES), 1)
    ii = lax.broadcasted_iota(jnp.int32, (c_sz, c_sz), 0)
    jj = lax.broadcasted_iota(jnp.int32, (c_sz, c_sz), 1)
    causal = ii >= jj
    strict = ii > jj
    eye = jnp.where(ii == jj, 1.0, 0.0).astype(F32)
    onorm = on_ref[...]

    for e in range(2):
        head = 2 * kh + e
        beta_all = jnp.sum(jnp.where(lane == head, gates, 0.0), axis=-1, keepdims=True)
        gc_all = jnp.sum(jnp.where(lane == GDN_V_HEADS + head, gates, 0.0),
                         axis=-1, keepdims=True)
        gr_all = gr_ref[pl.ds(GDN_V_HEADS + head, 1), :]
        vcols = slice(e * GDN_HEAD, (e + 1) * GDN_HEAD)
        for c in range(tb // c_sz):
            rows = slice(c * c_sz, (c + 1) * c_sz)
            qc = q_ref[rows, :]
            kc = k_ref[rows, :]
            vc = v_ref[rows, vcols]
            kcb = kc.astype(BF16)
            kk = lax.dot_general(kcb, kcb, (((1,), (1,)), ((), ())),
                                 preferred_element_type=F32)
            qk = lax.dot_general(qc.astype(BF16), kcb, (((1,), (1,)), ((), ())),
                                 preferred_element_type=F32)
            beta = beta_all[rows, :]
            gcol = gc_all[rows, :]
            grow = gr_all[:, rows]
            glast = gcol[c_sz - 1:c_sz, :]
            decay = jnp.where(causal, jnp.exp(jnp.minimum(gcol - grow, 0.0)), 0.0)
            a_mat = jnp.where(strict, beta * kk * decay, 0.0)
            p_pow = -a_mat
            t_inv = eye + p_pow
            for _ in range(int(math.log2(c_sz)) - 1):
                pb = p_pow.astype(BF16)
                p_pow = _dot(pb, pb)
                t_inv = t_inv + _dot(t_inv.astype(BF16), p_pow.astype(BF16))
            eg = jnp.exp(gcol)
            rhs = jnp.concatenate([vc * beta, kc * (beta * eg)], axis=1).astype(BF16)
            uw = _dot(t_inv.astype(BF16), rhs)
            u = uw[:, :GDN_HEAD]
            w = uw[:, GDN_HEAD:]
            state = s_scr[e]
            sb = state.astype(BF16)
            lhs = jnp.concatenate([w.astype(BF16), (qc * eg).astype(BF16)], axis=0)
            r = _dot(lhs, sb)
            v_new = u - r[:c_sz, :]
            vnb = v_new.astype(BF16)
            qkm = jnp.where(causal, qk * decay, 0.0).astype(BF16)
            o = r[c_sz:, :] + _dot(qkm, vnb)
            kd = (kc * jnp.exp(glast - gcol)).astype(BF16)
            s_scr[e] = state * jnp.exp(glast) + lax.dot_general(
                kd, vnb, (((0,), (0,)), ((), ())), preferred_element_type=F32)
            ms = jnp.mean(o * o, axis=-1, keepdims=True)
            o_ref[rows, vcols] = (o * lax.rsqrt(ms + NORM_EPS) * onorm
                                  * z_ref[rows, vcols]).astype(BF16)


def _gdn_core(qkv, zs, gates_c, gates_r, o_norm_row, *, tb):
    s = qkv.shape[0]
    nk = GDN_K_HEADS
    return pl.pallas_call(
        functools.partial(_gdn_core_kernel, tb=tb),
        grid=(nk, s // tb),
        in_specs=[
            pl.BlockSpec((tb, GDN_HEAD), lambda kh, t: (t, kh)),
            pl.BlockSpec((tb, GDN_HEAD), lambda kh, t: (t, nk + kh)),
            pl.BlockSpec((tb, 2 * GDN_HEAD), lambda kh, t: (t, nk + kh)),
            pl.BlockSpec((tb, 2 * GDN_HEAD), lambda kh, t: (t, kh)),
            pl.BlockSpec((tb, GATE_LANES), lambda kh, t: (t, 0)),
            pl.BlockSpec((GATE_LANES, tb), lambda kh, t: (0, t)),
            pl.BlockSpec((1, GDN_HEAD), lambda kh, t: (0, 0)),
        ],
        out_specs=pl.BlockSpec((tb, 2 * GDN_HEAD), lambda kh, t: (t, kh)),
        out_shape=jax.ShapeDtypeStruct((s, GDN_VAL_DIM), BF16),
        scratch_shapes=[pltpu.VMEM((2, GDN_HEAD, GDN_HEAD), F32)],
        compiler_params=_cparams(2),
        name="gdn_core",
    )(qkv, qkv, qkv, zs, gates_c, gates_r, o_norm_row)


def _res_matmul_kernel(h_ref, a_ref, w_ref, o_ref):
    o_ref[...] = h_ref[...] + _dot(a_ref[...], w_ref[...])


def _res_matmul(h, a, w, *, tm, tn):
    s, d = h.shape
    kdim = a.shape[1]
    return pl.pallas_call(
        _res_matmul_kernel,
        grid=(s // tm, d // tn),
        in_specs=[
            pl.BlockSpec((tm, tn), lambda i, j: (i, j)),
            pl.BlockSpec((tm, kdim), lambda i, j: (i, 0)),
            pl.BlockSpec((kdim, tn), lambda i, j: (0, j)),
        ],
        out_specs=pl.BlockSpec((tm, tn), lambda i, j: (i, j)),
        out_shape=jax.ShapeDtypeStruct((s, d), F32),
        compiler_params=_cparams(2),
        name="res_matmul",
    )(h, a, w)


def _norm_matmul_kernel(h_ref, g_ref, w_ref, o_ref, ob_ref, hn_scr):
    @pl.when(pl.program_id(1) == 0)
    def _():
        hn_scr[...] = _rms_rows(h_ref[...], g_ref[...]).astype(BF16)

    u = _dot(hn_scr[...], w_ref[...])
    o_ref[...] = u
    ob_ref[...] = u.astype(BF16)


def _norm_matmul(h, gain, w, *, tm, tn):
    s, d = h.shape
    n = w.shape[1]
    return pl.pallas_call(
        _norm_matmul_kernel,
        grid=(s // tm, n // tn),
        in_specs=[
            pl.BlockSpec((tm, d), lambda i, j: (i, 0)),
            pl.BlockSpec((1, d), lambda i, j: (0, 0)),
            pl.BlockSpec((d, tn), lambda i, j: (0, j)),
        ],
        out_specs=[
            pl.BlockSpec((tm, tn), lambda i, j: (i, j)),
            pl.BlockSpec((tm, tn), lambda i, j: (i, j)),
        ],
        out_shape=[
            jax.ShapeDtypeStruct((s, n), F32),
            jax.ShapeDtypeStruct((s, n), BF16),
        ],
        scratch_shapes=[pltpu.VMEM((tm, d), BF16)],
        compiler_params=_cparams(2),
        name="s5_in_proj",
    )(h, gain, w)


def _s5_core_kernel(u_ref, wcat_ref, cst_ref, lam_a_ref, lam_b_ref, y_ref, *, n_blocks):
    u = u_ref[0]
    uw = _dot(u, wcat_ref[0])
    y_intra = uw[:, :S5_BLOCK_W]
    x = uw[:, S5_BLOCK_W:]
    row = lax.broadcasted_iota(jnp.int32, (n_blocks, 2 * S5_STATE), 0)
    lam_a = lam_a_ref[0]
    lam_b = lam_b_ref[0]
    for r in range(int(math.log2(n_blocks))):
        d = 1 << r
        sh = jnp.where(row >= d, pltpu.roll(x, d, 0), 0.0)
        x = x + sh * lam_a[r:r + 1, :] + pltpu.roll(sh, S5_STATE, 1) * lam_b[r:r + 1, :]
    x_prev = jnp.where(row >= 1, pltpu.roll(x, 1, 0), 0.0)
    y_ref[0] = y_intra + _dot(x_prev.astype(BF16), cst_ref[0])


def _s5_core(u_blocks, wcat, cst, lam_a, lam_b):
    g, n_blocks, _ = u_blocks.shape
    levels = lam_a.shape[1]
    return pl.pallas_call(
        functools.partial(_s5_core_kernel, n_blocks=n_blocks),
        grid=(g,),
        in_specs=[
            pl.BlockSpec((1, n_blocks, S5_BLOCK_W), lambda i: (i, 0, 0)),
            pl.BlockSpec((1, S5_BLOCK_W, S5_BLOCK_W + 2 * S5_STATE), lambda i: (i, 0, 0)),
            pl.BlockSpec((1, 2 * S5_STATE, S5_BLOCK_W), lambda i: (i, 0, 0)),
            pl.BlockSpec((1, levels, 2 * S5_STATE), lambda i: (i, 0, 0)),
            pl.BlockSpec((1, levels, 2 * S5_STATE), lambda i: (i, 0, 0)),
        ],
        out_specs=pl.BlockSpec((1, n_blocks, S5_BLOCK_W), lambda i: (i, 0, 0)),
        out_shape=jax.ShapeDtypeStruct((g, n_blocks, S5_BLOCK_W), F32),
        compiler_params=_cparams(1),
        name="s5_core",
    )(u_blocks, wcat, cst, lam_a, lam_b)


def _s5_operators(lam_re, lam_im, log_step, b_re, b_im, c_re, c_im, n_blocks):
    g = lam_re.shape[0]
    lam = lax.complex(lam_re, lam_im)
    dt = jnp.exp(log_step)[:, None]
    lam_dt = lam * dt
    lam_bar = jnp.exp(lam_dt)
    b_bar = lax.complex(b_re, b_im) * ((lam_bar - 1.0) / lam)[..., None]
    cc = lax.complex(c_re, c_im)
    taus = jnp.arange(S5_BLOCK + 1, dtype=F32)
    powers = jnp.exp(lam_dt[:, None, :] * taus[None, :, None])
    kern = jnp.einsum('gcp,gtp,gpd->gtcd', cc, powers[:, :S5_BLOCK], b_bar).real
    s_idx = jnp.arange(S5_BLOCK)
    lag = s_idx[None, :] - s_idx[:, None]
    toe = jnp.where((lag >= 0)[None, :, :, None, None],
                    kern[:, jnp.clip(lag, 0, S5_BLOCK - 1)], 0.0)
    toe = toe.transpose(0, 1, 4, 2, 3).reshape(g, S5_BLOCK_W, S5_BLOCK_W)
    bst = powers[:, S5_BLOCK - 1 - s_idx, :, None] * b_bar[:, None]
    bst = bst.transpose(0, 1, 3, 2).reshape(g, S5_BLOCK_W, S5_STATE)
    wcat = jnp.concatenate([toe, bst.real, bst.imag], axis=-1).astype(BF16)
    cpow = cc[:, None] * powers[:, 1:S5_BLOCK + 1, None, :]
    cpow = cpow.transpose(0, 3, 1, 2).reshape(g, S5_STATE, S5_BLOCK_W)
    cst = jnp.concatenate([cpow.real, -cpow.imag], axis=1).astype(BF16)
    levels = int(math.log2(n_blocks))
    steps = (S5_BLOCK * (2.0 ** jnp.arange(levels, dtype=F32)))
    lam_pow = jnp.exp(lam_dt[:, None, :] * steps[None, :, None])
    lam_a = jnp.concatenate([lam_pow.real, lam_pow.real], axis=-1)
    lam_b = jnp.concatenate([-lam_pow.imag, lam_pow.imag], axis=-1)
    return wcat, cst, lam_a, lam_b


def _s5_out_kernel(h_ref, y_ref, u_ref, d_ref, wa_ref, wb_ref, o_ref, a_scr, *, tn):
    j = pl.program_id(1)

    @pl.when(j == 0)
    def _():
        a_scr[...] = _gelu_tanh(y_ref[...] + d_ref[...] * u_ref[...]).astype(BF16)

    a = a_scr[...]
    val = _dot(a, wa_ref[...])
    gate = _dot(a, wb_ref[...])
    cols = pl.ds(pl.multiple_of(j * tn, tn), tn)
    o_ref[...] = h_ref[:, cols] + val * _sigmoid(gate)


def _s5_out(h, y, u, d_row, w_out, *, tm, tn):
    s, d = h.shape
    nb = d // tn
    return pl.pallas_call(
        functools.partial(_s5_out_kernel, tn=tn),
        grid=(s // tm, nb),
        in_specs=[
            pl.BlockSpec((tm, d), lambda i, j: (i, 0)),
            pl.BlockSpec((tm, d), lambda i, j: (i, 0)),
            pl.BlockSpec((tm, d), lambda i, j: (i, 0)),
            pl.BlockSpec((1, d), lambda i, j: (0, 0)),
            pl.BlockSpec((d, tn), lambda i, j: (0, j)),
            pl.BlockSpec((d, tn), lambda i, j: (0, nb + j)),
        ],
        out_specs=pl.BlockSpec((tm, tn), lambda i, j: (i, j)),
        out_shape=jax.ShapeDtypeStruct((s, d), F32),
        scratch_shapes=[pltpu.VMEM((tm, d), BF16)],
        compiler_params=_cparams(2),
        name="s5_out",
    )(h, y, u, d_row, w_out, w_out)


def _mlp_kernel(h_ref, g_ref, wu_ref, wd_ref, o_ref, hn_scr, acc_scr):
    k = pl.program_id(1)

    @pl.when(k == 0)
    def _():
        hn_scr[...] = _rms_rows(h_ref[...], g_ref[...]).astype(BF16)
        acc_scr[...] = jnp.zeros_like(acc_scr)

    a = jnp.maximum(_dot(hn_scr[...], wu_ref[...]), 0.0)
    acc_scr[...] += _dot((a * a).astype(BF16), wd_ref[...])

    @pl.when(k == pl.num_programs(1) - 1)
    def _():
        o_ref[...] = h_ref[...] + acc_scr[...]


def _mlp(h, gain, w_up, w_down, *, tm, tf):
    s, d = h.shape
    ff = w_up.shape[1]
    return pl.pallas_call(
        _mlp_kernel,
        grid=(s // tm, ff // tf),
        in_specs=[
            pl.BlockSpec((tm, d), lambda i, k: (i, 0)),
            pl.BlockSpec((1, d), lambda i, k: (0, 0)),
            pl.BlockSpec((d, tf), lambda i, k: (0, k)),
            pl.BlockSpec((tf, d), lambda i, k: (k, 0)),
        ],
        out_specs=pl.BlockSpec((tm, d), lambda i, k: (i, 0)),
        out_shape=jax.ShapeDtypeStruct((s, d), F32),
        scratch_shapes=[pltpu.VMEM((tm, d), BF16), pltpu.VMEM((tm, d), F32)],
        compiler_params=_cparams(2),
        name="mlp",
    )(h, gain, w_up, w_down)


def _ple_kernel(h_ref, g_ref, p_ref, wg_ref, wp_ref, o_ref, hn_scr, pb_scr, *, tn):
    j = pl.program_id(1)

    @pl.when(j == 0)
    def _():
        hn_scr[...] = _rms_rows(h_ref[...], g_ref[...]).astype(BF16)
        pb_scr[...] = p_ref[...].astype(BF16)

    gate = _sigmoid(_dot(hn_scr[...], wg_ref[...]))
    emb = _dot(pb_scr[...], wp_ref[...])
    cols = pl.ds(pl.multiple_of(j * tn, tn), tn)
    o_ref[...] = h_ref[:, cols] + gate * emb


def _ple(h, gain, p, w_gate, w_proj, *, tm, tn):
    s, d = h.shape
    pd = p.shape[1]
    return pl.pallas_call(
        functools.partial(_ple_kernel, tn=tn),
        grid=(s // tm, d // tn),
        in_specs=[
            pl.BlockSpec((tm, d), lambda i, j: (i, 0)),
            pl.BlockSpec((1, d), lambda i, j: (0, 0)),
            pl.BlockSpec((tm, pd), lambda i, j: (i, 0)),
            pl.BlockSpec((d, tn), lambda i, j: (0, j)),
            pl.BlockSpec((pd, tn), lambda i, j: (0, j)),
        ],
        out_specs=pl.BlockSpec((tm, tn), lambda i, j: (i, j)),
        out_shape=jax.ShapeDtypeStruct((s, d), F32),
        scratch_shapes=[pltpu.VMEM((tm, d), BF16), pltpu.VMEM((tm, pd), BF16)],
        compiler_params=_cparams(2),
        name="ple",
    )(h, gain, p, w_gate, w_proj)


def _final_norm_kernel(h_ref, g_ref, o_ref):
    o_ref[...] = _rms_rows(h_ref[...], g_ref[...])


def _final_norm(h, gain, *, tm):
    s, d = h.shape
    return pl.pallas_call(
        _final_norm_kernel,
        grid=(s // tm,),
        in_specs=[pl.BlockSpec((tm, d), lambda i: (i, 0)),
                  pl.BlockSpec((1, d), lambda i: (0, 0))],
        out_specs=pl.BlockSpec((tm, d), lambda i: (i, 0)),
        out_shape=jax.ShapeDtypeStruct((s, d), F32),
        compiler_params=_cparams(1),
        name="final_norm",
    )(h, gain)


def _tile(n, want):
    t = min(n, want)
    while n % t:
        t //= 2
    return t


def _gdn_layer(h, gain, w_in, conv_w, a_log, dt_bias, o_norm, w_out):
    s, d = h.shape
    tm = _tile(s, 1024)
    w_in = w_in.astype(BF16)
    w_qkv = w_in[:, :GDN_CONV_DIM]
    w_z = w_in[:, GDN_CONV_DIM:GDN_CONV_DIM + GDN_VAL_DIM]
    w_ba = jnp.pad(w_in[:, GDN_CONV_DIM + GDN_VAL_DIM:],
                   ((0, 0), (0, GATE_LANES - 2 * GDN_V_HEADS)))
    pad_v = GATE_LANES - 2 * GDN_V_HEADS
    alog_row = jnp.concatenate([jnp.zeros((GDN_V_HEADS,), F32), a_log,
                                jnp.zeros((pad_v,), F32)])[None, :]
    dtb_row = jnp.concatenate([jnp.zeros((GDN_V_HEADS,), F32), dt_bias,
                               jnp.zeros((pad_v,), F32)])[None, :]
    qkv = _gdn_qkv(h, gain, w_qkv, conv_w, tm=tm, tn=512)
    zs, gates_c = _gdn_zgate(h, gain, w_z, w_ba, alog_row, dtb_row, tm=tm, tn=512)
    gates_r = gates_c.T
    on = _gdn_core(qkv, zs, gates_c, gates_r, o_norm[None, :], tb=_tile(s, 256))
    return _res_matmul(h, on, w_out.astype(BF16), tm=tm, tn=512)


def _s5_layer(h, gain, w_in, lam_re, lam_im, log_step, b_re, b_im, c_re, c_im, d_skip, w_out):
    s, d = h.shape
    g = d // S5_GROUP_CH
    n_blocks = s // S5_BLOCK
    tm = _tile(s, 1024)
    u, ub = _norm_matmul(h, gain, w_in.astype(BF16), tm=tm, tn=512)
    wcat, cst, lam_a, lam_b = _s5_operators(lam_re, lam_im, log_step, b_re, b_im,
                                            c_re, c_im, n_blocks)
    u_blocks = ub.reshape(n_blocks, S5_BLOCK, g, S5_GROUP_CH).transpose(2, 0, 1, 3)
    u_blocks = u_blocks.reshape(g, n_blocks, S5_BLOCK_W)
    y_blocks = _s5_core(u_blocks, wcat, cst, lam_a, lam_b)
    y = y_blocks.reshape(g, n_blocks, S5_BLOCK, S5_GROUP_CH).transpose(1, 2, 0, 3)
    y = y.reshape(s, d)
    return _s5_out(h, y, u, d_skip[None, :], w_out.astype(BF16), tm=_tile(s, 512), tn=512)


def kernel(x, p, norm_mix, norm_mlp, norm_ple, norm_final, gdn_w_in, gdn_conv_w, gdn_a_log,
           gdn_dt_bias, gdn_o_norm, gdn_w_out, s5_w_in, s5_lam_re, s5_lam_im, s5_log_step,
           s5_b_re, s5_b_im, s5_c_re, s5_c_im, s5_d, s5_w_out, mlp_w_up, mlp_w_down,
           ple_w_proj, ple_w_gate):
    bsz, s, d = x.shape
    depth = norm_mix.shape[0]
    outs = []
    for b in range(bsz):
        h = x[b]
        for i in range(depth):
            j = i // 2
            if i % 2 == 0:
                h = _gdn_layer(h, norm_mix[i][None, :], gdn_w_in[j], gdn_conv_w[j], gdn_a_log[j],
                               gdn_dt_bias[j], gdn_o_norm[j], gdn_w_out[j])
            else:
                h = _s5_layer(h, norm_mix[i][None, :], s5_w_in[j], s5_lam_re[j], s5_lam_im[j],
                              s5_log_step[j], s5_b_re[j], s5_b_im[j], s5_c_re[j], s5_c_im[j],
                              s5_d[j], s5_w_out[j])
            h = _mlp(h, norm_mlp[i][None, :], mlp_w_up[i].astype(BF16),
                     mlp_w_down[i].astype(BF16), tm=_tile(s, 512), tf=512)
            h = _ple(h, norm_ple[i][None, :], p[i, b], ple_w_gate[i].astype(BF16),
                     ple_w_proj[i].astype(BF16), tm=_tile(s, 512), tn=512)
        outs.append(_final_norm(h, norm_final[None, :], tm=_tile(s, 512)))
    return jnp.stack(outs, axis=0)
```

```python
import functools
import math

import jax
import jax.numpy as jnp
from jax import lax
from jax.experimental import pallas as pl
from jax.experimental.pallas import tpu as pltpu

F32 = jnp.float32
BF16 = jnp.bfloat16

NORM_EPS = 1e-6
L2_EPS = 1e-6

GDN_K_HEADS = 16
GDN_V_HEADS = 32
GDN_HEAD = 128
GDN_CONV = 4
GDN_KEY_DIM = GDN_K_HEADS * GDN_HEAD
GDN_VAL_DIM = GDN_V_HEADS * GDN_HEAD
GDN_CONV_DIM = 2 * GDN_KEY_DIM + GDN_VAL_DIM
GDN_CHUNK = 64
GDN_GROUP = 4 * GDN_CHUNK
GATE_LANES = 128

S5_GROUP_CH = 16
S5_STATE = 64
S5_BLOCK = 16
S5_BLOCK_W = S5_BLOCK * S5_GROUP_CH

SUBLANES = 8
LANES = 128
S5_GROUPS_PER_TILE = LANES // S5_GROUP_CH
VMEM_LIMIT_BYTES = 56 * 1024 * 1024


def _cparams(n_axes):
    return pltpu.CompilerParams(
        dimension_semantics=("arbitrary",) * n_axes,
        vmem_limit_bytes=VMEM_LIMIT_BYTES)


def _rms_rows(x, gain):
    ms = jnp.mean(x * x, axis=-1, keepdims=True)
    return x * lax.rsqrt(ms + NORM_EPS) * gain


def _sigmoid(x):
    return 1.0 / (1.0 + jnp.exp(-x))


def _softplus(x):
    return jnp.maximum(x, 0.0) + jnp.log(1.0 + jnp.exp(-jnp.abs(x)))


def _gelu_tanh(x):
    c = math.sqrt(2.0 / math.pi)
    return 0.5 * x * (1.0 + jnp.tanh(c * (x + 0.044715 * (x * x * x))))


def _dot(a, b):
    return jnp.dot(a, b, preferred_element_type=F32)


def _gdn_qkv_kernel(h_ref, g_ref, w_ref, cw_ref, o_ref, hn_scr, tail_scr, buf_scr,
                    *, tm, tn, n_qk_blocks):
    i = pl.program_id(0)
    j = pl.program_id(1)

    @pl.when(j == 0)
    def _():
        hn_scr[...] = _rms_rows(h_ref[...], g_ref[...]).astype(BF16)

    @pl.when(i == 0)
    def _():
        tail_scr[j] = jnp.zeros((SUBLANES, tn), F32)

    acc = _dot(hn_scr[...], w_ref[...])
    buf_scr[0:SUBLANES, :] = tail_scr[j]
    buf_scr[SUBLANES:, :] = acc
    tail_scr[j] = acc[tm - SUBLANES:, :]
    cw = cw_ref[...]
    y = acc * cw[3:4, :]
    for tap in range(GDN_CONV - 1):
        shift = GDN_CONV - 1 - tap
        y = y + buf_scr[pl.ds(SUBLANES - shift, tm), :] * cw[tap:tap + 1, :]
    y = y * _sigmoid(y)

    @pl.when(j >= n_qk_blocks)
    def _():
        o_ref[...] = y

    @pl.when(j < n_qk_blocks)
    def _():
        scale = jnp.where(j < n_qk_blocks // 2, GDN_HEAD ** -0.5, 1.0).astype(F32)
        for hh in range(tn // GDN_HEAD):
            sl = slice(hh * GDN_HEAD, (hh + 1) * GDN_HEAD)
            xs = y[:, sl]
            inv = lax.rsqrt(jnp.sum(xs * xs, axis=-1, keepdims=True) + L2_EPS)
            o_ref[:, sl] = xs * (inv * scale)


def _gdn_qkv(h, gain, w_in, conv_w, *, tm, tn):
    s, d = h.shape
    n = conv_w.shape[1]
    kern = functools.partial(_gdn_qkv_kernel, tm=tm, tn=tn,
                             n_qk_blocks=2 * GDN_KEY_DIM // tn)
    return pl.pallas_call(
        kern,
        grid=(s // tm, n // tn),
        in_specs=[
            pl.BlockSpec((tm, d), lambda i, j: (i, 0)),
            pl.BlockSpec((1, d), lambda i, j: (0, 0)),
            pl.BlockSpec((d, tn), lambda i, j: (0, j)),
            pl.BlockSpec((GDN_CONV, tn), lambda i, j: (0, j)),
        ],
        out_specs=pl.BlockSpec((tm, tn), lambda i, j: (i, j)),
        out_shape=jax.ShapeDtypeStruct((s, n), F32),
        scratch_shapes=[
            pltpu.VMEM((tm, d), BF16),
            pltpu.VMEM((n // tn, SUBLANES, tn), F32),
            pltpu.VMEM((tm + SUBLANES, tn), F32),
        ],
        compiler_params=_cparams(2),
        name="gdn_qkv",
    )(h, gain, w_in, conv_w)


def _gdn_zgate_kernel(h_ref, g_ref, wz_ref, wba_ref, alog_ref, dtb_ref, tri_ref,
                      z_ref, gate_ref, hn_scr, *, tm):
    j = pl.program_id(1)

    @pl.when(j == 0)
    def _():
        hn = _rms_rows(h_ref[...], g_ref[...]).astype(BF16)
        hn_scr[...] = hn
        ba = _dot(hn, wba_ref[...])
        beta = _sigmoid(ba)
        g = -jnp.exp(alog_ref[...]) * _softplus(ba + dtb_ref[...])
        lane = lax.broadcasted_iota(jnp.int32, (GDN_CHUNK, GATE_LANES), 1)
        tri = tri_ref[...]
        for c in range(tm // GDN_CHUNK):
            rows = slice(c * GDN_CHUNK, (c + 1) * GDN_CHUNK)
            gc = jnp.dot(tri, g[rows, :], preferred_element_type=F32,
                         precision=lax.Precision.HIGHEST)
            gate_ref[rows, :] = jnp.where(lane < GDN_V_HEADS, beta[rows, :], gc)

    z = _dot(hn_scr[...], wz_ref[...])
    z_ref[...] = z * _sigmoid(z)


def _gdn_zgate(h, gain, w_in, w_ba, alog_row, dtb_row, *, tm, tn):
    s, d = h.shape
    n = GDN_VAL_DIM
    z_off = GDN_CONV_DIM // tn
    tri = jnp.tril(jnp.ones((GDN_CHUNK, GDN_CHUNK), F32))
    return pl.pallas_call(
        functools.partial(_gdn_zgate_kernel, tm=tm),
        grid=(s // tm, n // tn),
        in_specs=[
            pl.BlockSpec((tm, d), lambda i, j: (i, 0)),
            pl.BlockSpec((1, d), lambda i, j: (0, 0)),
            pl.BlockSpec((d, tn), lambda i, j: (0, z_off + j)),
            pl.BlockSpec((d, GATE_LANES), lambda i, j: (0, 0)),
            pl.BlockSpec((1, GATE_LANES), lambda i, j: (0, 0)),
            pl.BlockSpec((1, GATE_LANES), lambda i, j: (0, 0)),
            pl.BlockSpec((GDN_CHUNK, GDN_CHUNK), lambda i, j: (0, 0)),
        ],
        out_specs=[
            pl.BlockSpec((tm, tn), lambda i, j: (i, j)),
            pl.BlockSpec((tm, GATE_LANES), lambda i, j: (i, 0)),
        ],
        out_shape=[
            jax.ShapeDtypeStruct((s, n), F32),
            jax.ShapeDtypeStruct((s, GATE_LANES), F32),
        ],
        scratch_shapes=[pltpu.VMEM((tm, d), BF16)],
        compiler_params=_cparams(2),
        name="gdn_zgate",
    )(h, gain, w_in, w_ba, alog_row, dtb_row, tri)


def _gdn_core_kernel(q_ref, k_ref, v_ref, z_ref, gc_ref, gr_ref, on_ref, o_ref, s_scr,
                     *, tb):
    kh = pl.program_id(0)
    t = pl.program_id(1)
    c_sz = GDN_CHUNK
    grp = GDN_GROUP
    n_in_grp = grp // c_sz

    @pl.when(t == 0)
    def _():
        s_scr[...] = jnp.zeros_like(s_scr)

    gates = gc_ref[...]
    lane = lax.broadcasted_iota(jnp.int32, (tb, GATE_LANES), 1)
    ri = lax.broadcasted_iota(jnp.int32, (grp, grp), 0)
    ci = lax.broadcasted_iota(jnp.int32, (grp, grp), 1)
    same_chunk = (ri // c_sz) == (ci // c_sz)
    bd_causal = same_chunk & (ri >= ci)
    bd_strict = same_chunk & (ri > ci)
    bd_mask = jnp.where(same_chunk, 1.0, 0.0).astype(BF16)
    pi = lax.broadcasted_iota(jnp.int32, (c_sz, grp), 0)
    pj = lax.broadcasted_iota(jnp.int32, (c_sz, grp), 1)
    eye_p = jnp.where(pi == (pj % c_sz), 1.0, 0.0).astype(F32)
    onorm = on_ref[...]
    nt_dims = (((1,), (1,)), ((), ()))
    tn_dims = (((0,), (0,)), ((), ()))

    def block_diag(packed):
        return jnp.concatenate([packed] * n_in_grp, axis=0) * bd_mask

    per_head = []
    for e in range(2):
        head = 2 * kh + e
        beta_all = jnp.sum(jnp.where(lane == head, gates, 0.0), axis=-1, keepdims=True)
        gc_all = jnp.sum(jnp.where(lane == GDN_V_HEADS + head, gates, 0.0),
                         axis=-1, keepdims=True)
        gr_all = gr_ref[pl.ds(GDN_V_HEADS + head, 1), :]
        per_head.append((beta_all, gc_all, gr_all))

    n_grp = tb // grp
    pairs = [(g, e) for g in range(n_grp) for e in range(2)]
    kgs, qgs, gkk, gqk = [], [], [], []
    for g in range(n_grp):
        rows = slice(g * grp, (g + 1) * grp)
        kg = k_ref[rows, :]
        qg = q_ref[rows, :]
        kb = kg.astype(BF16)
        kgs.append(kg)
        qgs.append(qg)
        gkk.append(lax.dot_general(kb, kb, nt_dims, preferred_element_type=F32))
        gqk.append(lax.dot_general(qg.astype(BF16), kb, nt_dims, preferred_element_type=F32))

    bcols, gcols, decays, t_invs, q_pows = [], [], [], [], []
    for g, e in pairs:
        rows = slice(g * grp, (g + 1) * grp)
        beta_all, gc_all, gr_all = per_head[e]
        bcol = beta_all[rows, :]
        gcol = gc_all[rows, :]
        decay = jnp.exp(jnp.minimum(gcol - gr_all[:, rows], 0.0))
        a_nat = jnp.where(bd_strict, bcol * gkk[g] * decay, 0.0)
        p_mat = -(a_nat[0:c_sz] + a_nat[c_sz:2 * c_sz]
                  + a_nat[2 * c_sz:3 * c_sz] + a_nat[3 * c_sz:4 * c_sz])
        bcols.append(bcol)
        gcols.append(gcol)
        decays.append(decay)
        t_invs.append(eye_p + p_mat)
        q_pows.append(p_mat)

    for i in range(len(pairs)):
        pb = q_pows[i].astype(BF16)
        q_pows[i] = _dot(pb, block_diag(pb))
    for _ in range(int(math.log2(c_sz)) - 2):
        for i in range(len(pairs)):
            tq = _dot(jnp.concatenate([t_invs[i], q_pows[i]], axis=0).astype(BF16),
                      block_diag(q_pows[i].astype(BF16)))
            t_invs[i] = t_invs[i] + tq[:c_sz]
            q_pows[i] = tq[c_sz:]
    for i in range(len(pairs)):
        t_invs[i] = t_invs[i] + _dot(t_invs[i].astype(BF16),
                                     block_diag(q_pows[i].astype(BF16)))

    egs, wus = [], []
    for i, (g, e) in enumerate(pairs):
        rows = slice(g * grp, (g + 1) * grp)
        vcols = slice(e * GDN_HEAD, (e + 1) * GDN_HEAD)
        eg = jnp.exp(gcols[i])
        rhs = jnp.concatenate([kgs[g] * (bcols[i] * eg), v_ref[rows, vcols] * bcols[i]],
                              axis=1).astype(BF16)
        egs.append(eg)
        wus.append(_dot(block_diag(t_invs[i].astype(BF16)), rhs).astype(BF16))
    qws = []
    for i, (g, e) in enumerate(pairs):
        qkm = jnp.where(bd_causal, gqk[g] * decays[i], 0.0).astype(BF16)
        qws.append(_dot(qkm, wus[i]))

    mq = [[], []]
    nn = [[], []]
    oi = [[], []]
    dl = [[], []]
    for g in range(n_grp):
        for m in range(n_in_grp):
            cr = slice(m * c_sz, (m + 1) * c_sz)
            for e in range(2):
                i = 2 * g + e
                gcol = gcols[i]
                glast = gcol[(m + 1) * c_sz - 1:(m + 1) * c_sz, :]
                kd = (kgs[g][cr] * jnp.exp(glast - gcol[cr])).astype(BF16)
                mn = lax.dot_general(kd, wus[i][cr], tn_dims,
                                     preferred_element_type=F32)
                q_eff = (qgs[g][cr] * egs[i][cr] - qws[i][cr, :GDN_HEAD]).astype(BF16)
                mq[e].append(jnp.concatenate([mn[:, :GDN_HEAD].astype(BF16), q_eff], axis=0))
                nn[e].append(mn[:, GDN_HEAD:])
                oi[e].append(qws[i][cr, GDN_HEAD:])
                dl[e].append(jnp.exp(glast))

    state = [s_scr[0], s_scr[1]]
    for c in range(tb // c_sz):
        rows = slice(c * c_sz, (c + 1) * c_sz)
        for e in range(2):
            vcols = slice(e * GDN_HEAD, (e + 1) * GDN_HEAD)
            r = _dot(mq[e][c], state[e].astype(BF16))
            o = r[GDN_HEAD:] + oi[e][c]
            state[e] = state[e] * dl[e][c] - r[:GDN_HEAD] + nn[e][c]
            ms = jnp.mean(o * o, axis=-1, keepdims=True)
            o_ref[rows, vcols] = (o * lax.rsqrt(ms + NORM_EPS) * onorm
                                  * z_ref[rows, vcols]).astype(BF16)
    s_scr[0] = state[0]
    s_scr[1] = state[1]


def _gdn_core(qkv, zs, gates_c, gates_r, o_norm_row, *, tb):
    s = qkv.shape[0]
    nk = GDN_K_HEADS
    return pl.pallas_call(
        functools.partial(_gdn_core_kernel, tb=tb),
        grid=(nk, s // tb),
        in_specs=[
            pl.BlockSpec((tb, GDN_HEAD), lambda kh, t: (t, kh)),
            pl.BlockSpec((tb, GDN_HEAD), lambda kh, t: (t, nk + kh)),
            pl.BlockSpec((tb, 2 * GDN_HEAD), lambda kh, t: (t, nk + kh)),
            pl.BlockSpec((tb, 2 * GDN_HEAD), lambda kh, t: (t, kh)),
            pl.BlockSpec((tb, GATE_LANES), lambda kh, t: (t, 0)),
            pl.BlockSpec((GATE_LANES, tb), lambda kh, t: (0, t)),
            pl.BlockSpec((1, GDN_HEAD), lambda kh, t: (0, 0)),
        ],
        out_specs=pl.BlockSpec((tb, 2 * GDN_HEAD), lambda kh, t: (t, kh)),
        out_shape=jax.ShapeDtypeStruct((s, GDN_VAL_DIM), BF16),
        scratch_shapes=[pltpu.VMEM((2, GDN_HEAD, GDN_HEAD), F32)],
        compiler_params=_cparams(2),
        name="gdn_core",
    )(qkv, qkv, qkv, zs, gates_c, gates_r, o_norm_row)


def _res_matmul_kernel(h_ref, a_ref, w_ref, o_ref):
    o_ref[...] = h_ref[...] + _dot(a_ref[...], w_ref[...])


def _res_matmul(h, a, w, *, tm, tn):
    s, d = h.shape
    kdim = a.shape[1]
    return pl.pallas_call(
        _res_matmul_kernel,
        grid=(s // tm, d // tn),
        in_specs=[
            pl.BlockSpec((tm, tn), lambda i, j: (i, j)),
            pl.BlockSpec((tm, kdim), lambda i, j: (i, 0)),
            pl.BlockSpec((kdim, tn), lambda i, j: (0, j)),
        ],
        out_specs=pl.BlockSpec((tm, tn), lambda i, j: (i, j)),
        out_shape=jax.ShapeDtypeStruct((s, d), F32),
        compiler_params=_cparams(2),
        name="res_matmul",
    )(h, a, w)


def _norm_matmul_kernel(h_ref, g_ref, w_ref, o_ref, hn_scr):
    @pl.when(pl.program_id(1) == 0)
    def _():
        hn_scr[...] = _rms_rows(h_ref[...], g_ref[...]).astype(BF16)

    o_ref[...] = _dot(hn_scr[...], w_ref[...])


def _norm_matmul(h, gain, w, *, tm, tn):
    s, d = h.shape
    n = w.shape[1]
    return pl.pallas_call(
        _norm_matmul_kernel,
        grid=(s // tm, n // tn),
        in_specs=[
            pl.BlockSpec((tm, d), lambda i, j: (i, 0)),
            pl.BlockSpec((1, d), lambda i, j: (0, 0)),
            pl.BlockSpec((d, tn), lambda i, j: (0, j)),
        ],
        out_specs=pl.BlockSpec((tm, tn), lambda i, j: (i, j)),
        out_shape=jax.ShapeDtypeStruct((s, n), F32),
        scratch_shapes=[pltpu.VMEM((tm, d), BF16)],
        compiler_params=_cparams(2),
        name="s5_in_proj",
    )(h, gain, w)


def _s5_core_kernel(u_ref, d_ref, wcat_ref, cst_ref, lam_a_ref, lam_b_ref, a_ref, *, n_blocks):
    gpt = S5_GROUPS_PER_TILE
    half = S5_BLOCK // 2
    piece = lax.broadcasted_iota(jnp.int32, (n_blocks, LANES), 1) // S5_GROUP_CH
    row = lax.broadcasted_iota(jnp.int32, (n_blocks, 2 * S5_STATE), 0)

    def lane_roll(x, shift):
        shift = shift % LANES
        return x if shift == 0 else pltpu.roll(x, shift, 1)

    xs = [u_ref[pl.ds(s, n_blocks, stride=S5_BLOCK), :] for s in range(S5_BLOCK)]

    uvecs = []
    for g in range(gpt):
        halves = []
        for hh in range(2):
            acc = jnp.zeros((n_blocks, LANES), F32)
            for sp in range(half):
                moved = lane_roll(xs[hh * half + sp], S5_GROUP_CH * (sp - g))
                acc = jnp.where(piece == sp, moved, acc)
            halves.append(acc)
        uvecs.append(jnp.concatenate(halves, axis=1).astype(BF16))

    uws = [_dot(uvecs[g], wcat_ref[g]) for g in range(gpt)]
    states = [uw[:, S5_BLOCK_W:] for uw in uws]
    for r in range(int(math.log2(n_blocks))):
        dist = 1 << r
        for g in range(gpt):
            x = states[g]
            la = lam_a_ref[g, r:r + 1, :]
            lb = lam_b_ref[g, r:r + 1, :]
            sh = jnp.where(row >= dist, pltpu.roll(x, dist, 0), 0.0)
            states[g] = x + sh * la + pltpu.roll(sh, S5_STATE, 1) * lb
    ys = []
    for g in range(gpt):
        x_prev = jnp.where(row >= 1, pltpu.roll(states[g], 1, 0), 0.0)
        ys.append(uws[g][:, :S5_BLOCK_W] + _dot(x_prev.astype(BF16), cst_ref[g]))

    dsk = d_ref[...]
    for tt in range(S5_BLOCK):
        hh, tp = divmod(tt, half)
        acc = jnp.zeros((n_blocks, LANES), F32)
        for g in range(gpt):
            moved = lane_roll(ys[g][:, hh * LANES:(hh + 1) * LANES], S5_GROUP_CH * (g - tp))
            acc = jnp.where(piece == g, moved, acc)
        a_ref[pl.ds(tt, n_blocks, stride=S5_BLOCK), :] = _gelu_tanh(acc + dsk * xs[tt])


def _s5_core(u, d_row, wcat, cst, lam_a, lam_b):
    s, d = u.shape
    n_blocks = s // S5_BLOCK
    gpt = S5_GROUPS_PER_TILE
    levels = lam_a.shape[1]
    return pl.pallas_call(
        functools.partial(_s5_core_kernel, n_blocks=n_blocks),
        grid=(d // LANES,),
        in_specs=[
            pl.BlockSpec((s, LANES), lambda i: (0, i)),
            pl.BlockSpec((1, LANES), lambda i: (0, i)),
            pl.BlockSpec((gpt, S5_BLOCK_W, S5_BLOCK_W + 2 * S5_STATE), lambda i: (i, 0, 0)),
            pl.BlockSpec((gpt, 2 * S5_STATE, S5_BLOCK_W), lambda i: (i, 0, 0)),
            pl.BlockSpec((gpt, levels, 2 * S5_STATE), lambda i: (i, 0, 0)),
            pl.BlockSpec((gpt, levels, 2 * S5_STATE), lambda i: (i, 0, 0)),
        ],
        out_specs=pl.BlockSpec((s, LANES), lambda i: (0, i)),
        out_shape=jax.ShapeDtypeStruct((s, d), F32),
        compiler_params=_cparams(1),
        name="s5_core",
    )(u, d_row, wcat, cst, lam_a, lam_b)


def _s5_operators(lam_re, lam_im, log_step, b_re, b_im, c_re, c_im, n_blocks):
    g = lam_re.shape[0]
    lam = lax.complex(lam_re, lam_im)
    dt = jnp.exp(log_step)[:, None]
    lam_dt = lam * dt
    lam_bar = jnp.exp(lam_dt)
    b_bar = lax.complex(b_re, b_im) * ((lam_bar - 1.0) / lam)[..., None]
    cc = lax.complex(c_re, c_im)
    taus = jnp.arange(S5_BLOCK + 1, dtype=F32)
    powers = jnp.exp(lam_dt[:, None, :] * taus[None, :, None])
    kern = jnp.einsum('gcp,gtp,gpd->gtcd', cc, powers[:, :S5_BLOCK], b_bar).real
    s_idx = jnp.arange(S5_BLOCK)
    lag = s_idx[None, :] - s_idx[:, None]
    toe = jnp.where((lag >= 0)[None, :, :, None, None],
                    kern[:, jnp.clip(lag, 0, S5_BLOCK - 1)], 0.0)
    toe = toe.transpose(0, 1, 4, 2, 3).reshape(g, S5_BLOCK_W, S5_BLOCK_W)
    bst = powers[:, S5_BLOCK - 1 - s_idx, :, None] * b_bar[:, None]
    bst = bst.transpose(0, 1, 3, 2).reshape(g, S5_BLOCK_W, S5_STATE)
    wcat = jnp.concatenate([toe, bst.real, bst.imag], axis=-1).astype(BF16)
    cpow = cc[:, None] * powers[:, 1:S5_BLOCK + 1, None, :]
    cpow = cpow.transpose(0, 3, 1, 2).reshape(g, S5_STATE, S5_BLOCK_W)
    cst = jnp.concatenate([cpow.real, -cpow.imag], axis=1).astype(BF16)
    levels = int(math.log2(n_blocks))
    steps = (S5_BLOCK * (2.0 ** jnp.arange(levels, dtype=F32)))
    lam_pow = jnp.exp(lam_dt[:, None, :] * steps[None, :, None])
    lam_a = jnp.concatenate([lam_pow.real, lam_pow.real], axis=-1)
    lam_b = jnp.concatenate([-lam_pow.imag, lam_pow.imag], axis=-1)
    return wcat, cst, lam_a, lam_b


def _s5_out_kernel(h_ref, a_ref, wa_ref, wb_ref, o_ref, a_scr):
    @pl.when(pl.program_id(1) == 0)
    def _():
        a_scr[...] = a_ref[...].astype(BF16)

    a = a_scr[...]
    o_ref[...] = h_ref[...] + _dot(a, wa_ref[...]) * _sigmoid(_dot(a, wb_ref[...]))


def _s5_out(h, a, w_out, *, tm, tn):
    s, d = h.shape
    nb = d // tn
    return pl.pallas_call(
        _s5_out_kernel,
        grid=(s // tm, nb),
        in_specs=[
            pl.BlockSpec((tm, tn), lambda i, j: (i, j)),
            pl.BlockSpec((tm, d), lambda i, j: (i, 0)),
            pl.BlockSpec((d, tn), lambda i, j: (0, j)),
            pl.BlockSpec((d, tn), lambda i, j: (0, nb + j)),
        ],
        out_specs=pl.BlockSpec((tm, tn), lambda i, j: (i, j)),
        out_shape=jax.ShapeDtypeStruct((s, d), F32),
        scratch_shapes=[pltpu.VMEM((tm, d), BF16)],
        compiler_params=_cparams(2),
        name="s5_out",
    )(h, a, w_out, w_out)


def _mlp_kernel(h_ref, g_ref, wu_ref, wd_ref, o_ref, hn_scr, acc_scr):
    k = pl.program_id(1)

    @pl.when(k == 0)
    def _():
        hn_scr[...] = _rms_rows(h_ref[...], g_ref[...]).astype(BF16)
        acc_scr[...] = jnp.zeros_like(acc_scr)

    a = jnp.maximum(_dot(hn_scr[...], wu_ref[...]), 0.0)
    acc_scr[...] += _dot((a * a).astype(BF16), wd_ref[...])

    @pl.when(k == pl.num_programs(1) - 1)
    def _():
        o_ref[...] = h_ref[...] + acc_scr[...]


def _mlp(h, gain, w_up, w_down, *, tm, tf):
    s, d = h.shape
    ff = w_up.shape[1]
    return pl.pallas_call(
        _mlp_kernel,
        grid=(s // tm, ff // tf),
        in_specs=[
            pl.BlockSpec((tm, d), lambda i, k: (i, 0)),
            pl.BlockSpec((1, d), lambda i, k: (0, 0)),
            pl.BlockSpec((d, tf), lambda i, k: (0, k)),
            pl.BlockSpec((tf, d), lambda i, k: (k, 0)),
        ],
        out_specs=pl.BlockSpec((tm, d), lambda i, k: (i, 0)),
        out_shape=jax.ShapeDtypeStruct((s, d), F32),
        scratch_shapes=[pltpu.VMEM((tm, d), BF16), pltpu.VMEM((tm, d), F32)],
        compiler_params=_cparams(2),
        name="mlp",
    )(h, gain, w_up, w_down)


def _ple_kernel(h_ref, g_ref, p_ref, wg_ref, wp_ref, o_ref, hn_scr, pb_scr, *, tn):
    j = pl.program_id(1)

    @pl.when(j == 0)
    def _():
        hn_scr[...] = _rms_rows(h_ref[...], g_ref[...]).astype(BF16)
        pb_scr[...] = p_ref[...].astype(BF16)

    gate = _sigmoid(_dot(hn_scr[...], wg_ref[...]))
    emb = _dot(pb_scr[...], wp_ref[...])
    cols = pl.ds(pl.multiple_of(j * tn, tn), tn)
    o_ref[...] = h_ref[:, cols] + gate * emb


def _ple(h, gain, p, layer, batch, w_gate, w_proj, *, tm, tn):
    s, d = h.shape
    pd = p.shape[-1]
    return pl.pallas_call(
        functools.partial(_ple_kernel, tn=tn),
        grid=(s // tm, d // tn),
        in_specs=[
            pl.BlockSpec((tm, d), lambda i, j: (i, 0)),
            pl.BlockSpec((1, d), lambda i, j: (0, 0)),
            pl.BlockSpec((None, None, tm, pd), lambda i, j: (layer, batch, i, 0)),
            pl.BlockSpec((d, tn), lambda i, j: (0, j)),
            pl.BlockSpec((pd, tn), lambda i, j: (0, j)),
        ],
        out_specs=pl.BlockSpec((tm, tn), lambda i, j: (i, j)),
        out_shape=jax.ShapeDtypeStruct((s, d), F32),
        scratch_shapes=[pltpu.VMEM((tm, d), BF16), pltpu.VMEM((tm, pd), BF16)],
        compiler_params=_cparams(2),
        name="ple",
    )(h, gain, p, w_gate, w_proj)


def _final_norm_kernel(h_ref, g_ref, o_ref):
    o_ref[...] = _rms_rows(h_ref[...], g_ref[...])


def _final_norm(h, gain, *, tm):
    s, d = h.shape
    return pl.pallas_call(
        _final_norm_kernel,
        grid=(s // tm,),
        in_specs=[pl.BlockSpec((tm, d), lambda i: (i, 0)),
                  pl.BlockSpec((1, d), lambda i: (0, 0))],
        out_specs=pl.BlockSpec((tm, d), lambda i: (i, 0)),
        out_shape=jax.ShapeDtypeStruct((s, d), F32),
        compiler_params=_cparams(1),
        name="final_norm",
    )(h, gain)


def _tile(n, want):
    t = min(n, want)
    while n % t:
        t //= 2
    return t


def _gdn_layer(h, gain, w_in, conv_w, a_log, dt_bias, o_norm, w_out):
    s, d = h.shape
    tm = _tile(s, 1024)
    w_ba = jnp.pad(w_in[:, GDN_CONV_DIM + GDN_VAL_DIM:].astype(BF16),
                   ((0, 0), (0, GATE_LANES - 2 * GDN_V_HEADS)))
    w_in = w_in.astype(BF16)
    pad_v = GATE_LANES - 2 * GDN_V_HEADS
    alog_row = jnp.concatenate([jnp.zeros((GDN_V_HEADS,), F32), a_log,
                                jnp.zeros((pad_v,), F32)])[None, :]
    dtb_row = jnp.concatenate([jnp.zeros((GDN_V_HEADS,), F32), dt_bias,
                               jnp.zeros((pad_v,), F32)])[None, :]
    qkv = _gdn_qkv(h, gain, w_in, conv_w, tm=tm, tn=512)
    zs, gates_c = _gdn_zgate(h, gain, w_in, w_ba, alog_row, dtb_row, tm=tm, tn=512)
    gates_r = gates_c.T
    on = _gdn_core(qkv, zs, gates_c, gates_r, o_norm[None, :], tb=_tile(s, 512))
    return _res_matmul(h, on, w_out.astype(BF16), tm=tm, tn=512)


def _s5_layer(h, gain, w_in, lam_re, lam_im, log_step, b_re, b_im, c_re, c_im, d_skip, w_out):
    s, d = h.shape
    u = _norm_matmul(h, gain, w_in.astype(BF16), tm=_tile(s, 1024), tn=512)
    wcat, cst, lam_a, lam_b = _s5_operators(lam_re, lam_im, log_step, b_re, b_im,
                                            c_re, c_im, s // S5_BLOCK)
    a = _s5_core(u, d_skip[None, :], wcat, cst, lam_a, lam_b)
    return _s5_out(h, a, w_out.astype(BF16), tm=_tile(s, 512), tn=512)


def kernel(x, p, norm_mix, norm_mlp, norm_ple, norm_final, gdn_w_in, gdn_conv_w, gdn_a_log,
           gdn_dt_bias, gdn_o_norm, gdn_w_out, s5_w_in, s5_lam_re, s5_lam_im, s5_log_step,
           s5_b_re, s5_b_im, s5_c_re, s5_c_im, s5_d, s5_w_out, mlp_w_up, mlp_w_down,
           ple_w_proj, ple_w_gate):
    bsz, s, d = x.shape
    depth = norm_mix.shape[0]
    outs = []
    for b in range(bsz):
        h = x[b]
        for i in range(depth):
            j = i // 2
            if i % 2 == 0:
                h = _gdn_layer(h, norm_mix[i][None, :], gdn_w_in[j], gdn_conv_w[j], gdn_a_log[j],
                               gdn_dt_bias[j], gdn_o_norm[j], gdn_w_out[j])
            else:
                h = _s5_layer(h, norm_mix[i][None, :], s5_w_in[j], s5_lam_re[j], s5_lam_im[j],
                              s5_log_step[j], s5_b_re[j], s5_b_im[j], s5_c_re[j], s5_c_im[j],
                              s5_d[j], s5_w_out[j])
            h = _mlp(h, norm_mlp[i][None, :], mlp_w_up[i].astype(BF16),
                     mlp_w_down[i].astype(BF16), tm=_tile(s, 512), tf=512)
            h = _ple(h, norm_ple[i][None, :], p, i, b, ple_w_gate[i].astype(BF16),
                     ple_w_proj[i].astype(BF16), tm=_tile(s, 512), tn=512)
        outs.append(_final_norm(h, norm_final[None, :], tm=_tile(s, 512)))
    return jnp.stack(outs, axis=0)
```

```python
import functools
import math

import jax
import jax.numpy as jnp
from jax import lax
from jax.experimental import pallas as pl
from jax.experimental.pallas import tpu as pltpu

F32 = jnp.float32
BF16 = jnp.bfloat16

NORM_EPS = 1e-6
L2_EPS = 1e-6

GDN_K_HEADS = 16
GDN_V_HEADS = 32
GDN_HEAD = 128
GDN_CONV = 4
GDN_KEY_DIM = GDN_K_HEADS * GDN_HEAD
GDN_VAL_DIM = GDN_V_HEADS * GDN_HEAD
GDN_CONV_DIM = 2 * GDN_KEY_DIM + GDN_VAL_DIM
GDN_CHUNK = 64
GDN_GROUP = 4 * GDN_CHUNK
QKV_ROW_SUBBLOCK = 256
GATE_LANES = 128

S5_GROUP_CH = 16
S5_STATE = 64
S5_BLOCK = 16
S5_BLOCK_W = S5_BLOCK * S5_GROUP_CH

SUBLANES = 8
LANES = 128
S5_GROUPS_PER_TILE = LANES // S5_GROUP_CH
S5_ROW_CHUNK = 16
S5_RELAYOUT_UNROLL = 4
VMEM_LIMIT_BYTES = 56 * 1024 * 1024


def _cparams(n_axes):
    return pltpu.CompilerParams(
        dimension_semantics=("arbitrary",) * n_axes,
        vmem_limit_bytes=VMEM_LIMIT_BYTES)


def _rms_rows(x, gain):
    ms = jnp.mean(x * x, axis=-1, keepdims=True)
    return x * lax.rsqrt(ms + NORM_EPS) * gain


def _sigmoid(x):
    return 1.0 / (1.0 + jnp.exp(-x))


def _softplus(x):
    return jnp.maximum(x, 0.0) + jnp.log(1.0 + jnp.exp(-jnp.abs(x)))


def _gelu_tanh(x):
    c = math.sqrt(2.0 / math.pi)
    return 0.5 * x * (1.0 + jnp.tanh(c * (x + 0.044715 * (x * x * x))))


def _dot(a, b):
    return jnp.dot(a, b, preferred_element_type=F32)


def _gdn_qkv_kernel(h_ref, g_ref, w_ref, cw_ref, o_ref, hn_scr, tail_scr, buf_scr,
                    *, tm, tn, rb, n_qk_blocks):
    i = pl.program_id(0)
    j = pl.program_id(1)

    @pl.when(j == 0)
    def _():
        hn_scr[...] = _rms_rows(h_ref[...], g_ref[...]).astype(BF16)

    @pl.when(i == 0)
    def _():
        tail_scr[j] = jnp.zeros((SUBLANES, tn), F32)

    buf_scr[0:SUBLANES, :] = tail_scr[j]
    cw = cw_ref[...]

    def row_blocks(l2_norm):
        if l2_norm:
            scale = jnp.where(j < n_qk_blocks // 2, GDN_HEAD ** -0.5, 1.0).astype(F32)
        for r in range(tm // rb):
            rows = slice(r * rb, (r + 1) * rb)
            acc = _dot(hn_scr[rows, :], w_ref[...])
            buf_scr[SUBLANES + r * rb:SUBLANES + (r + 1) * rb, :] = acc
            y = acc * cw[3:4, :]
            for tap in range(GDN_CONV - 1):
                shift = GDN_CONV - 1 - tap
                y = y + buf_scr[pl.ds(SUBLANES + r * rb - shift, rb), :] * cw[tap:tap + 1, :]
            y = y * _sigmoid(y)
            if not l2_norm:
                o_ref[rows, :] = y
                continue
            for hh in range(tn // GDN_HEAD):
                sl = slice(hh * GDN_HEAD, (hh + 1) * GDN_HEAD)
                xs = y[:, sl]
                inv = lax.rsqrt(jnp.sum(xs * xs, axis=-1, keepdims=True) + L2_EPS)
                o_ref[rows, sl] = xs * (inv * scale)
        tail_scr[j] = buf_scr[tm:tm + SUBLANES, :]

    pl.when(j < n_qk_blocks)(functools.partial(row_blocks, True))
    pl.when(j >= n_qk_blocks)(functools.partial(row_blocks, False))


def _gdn_qkv(h, gain, w_in, layer, conv_w, *, tm, tn):
    s, d = h.shape
    n = conv_w.shape[1]
    kern = functools.partial(_gdn_qkv_kernel, tm=tm, tn=tn, rb=_tile(tm, QKV_ROW_SUBBLOCK),
                             n_qk_blocks=2 * GDN_KEY_DIM // tn)
    return pl.pallas_call(
        kern,
        grid=(s // tm, n // tn),
        in_specs=[
            pl.BlockSpec((tm, d), lambda i, j: (i, 0)),
            pl.BlockSpec((1, d), lambda i, j: (0, 0)),
            pl.BlockSpec((None, d, tn), lambda i, j: (layer, 0, j)),
            pl.BlockSpec((GDN_CONV, tn), lambda i, j: (0, j)),
        ],
        out_specs=pl.BlockSpec((tm, tn), lambda i, j: (i, j)),
        out_shape=jax.ShapeDtypeStruct((s, n), F32),
        scratch_shapes=[
            pltpu.VMEM((tm, d), BF16),
            pltpu.VMEM((n // tn, SUBLANES, tn), F32),
            pltpu.VMEM((tm + SUBLANES, tn), F32),
        ],
        compiler_params=_cparams(2),
        name="gdn_qkv",
    )(h, gain, w_in, conv_w)


def _gdn_zgate_kernel(h_ref, g_ref, wz_ref, wba_ref, alog_ref, dtb_ref, tri_ref,
                      z_ref, gate_ref, hn_scr, *, tm):
    j = pl.program_id(1)

    @pl.when(j == 0)
    def _():
        hn = _rms_rows(h_ref[...], g_ref[...]).astype(BF16)
        hn_scr[...] = hn
        ba = _dot(hn, wba_ref[...])
        beta = _sigmoid(ba)
        g = -jnp.exp(alog_ref[...]) * _softplus(ba + dtb_ref[...])
        lane = lax.broadcasted_iota(jnp.int32, (GDN_CHUNK, GATE_LANES), 1)
        tri = tri_ref[...]
        for c in range(tm // GDN_CHUNK):
            rows = slice(c * GDN_CHUNK, (c + 1) * GDN_CHUNK)
            gc = jnp.dot(tri, g[rows, :], preferred_element_type=F32,
                         precision=lax.Precision.HIGHEST)
            gate_ref[rows, :] = jnp.where(lane < GDN_V_HEADS, beta[rows, :], gc)

    z = _dot(hn_scr[...], wz_ref[...])
    z_ref[...] = z * _sigmoid(z)


def _gdn_zgate(h, gain, w_in, layer, w_ba, alog_row, dtb_row, *, tm, tn):
    s, d = h.shape
    n = GDN_VAL_DIM
    z_off = GDN_CONV_DIM // tn
    tri = jnp.tril(jnp.ones((GDN_CHUNK, GDN_CHUNK), F32))
    return pl.pallas_call(
        functools.partial(_gdn_zgate_kernel, tm=tm),
        grid=(s // tm, n // tn),
        in_specs=[
            pl.BlockSpec((tm, d), lambda i, j: (i, 0)),
            pl.BlockSpec((1, d), lambda i, j: (0, 0)),
            pl.BlockSpec((None, d, tn), lambda i, j: (layer, 0, z_off + j)),
            pl.BlockSpec((d, GATE_LANES), lambda i, j: (0, 0)),
            pl.BlockSpec((1, GATE_LANES), lambda i, j: (0, 0)),
            pl.BlockSpec((1, GATE_LANES), lambda i, j: (0, 0)),
            pl.BlockSpec((GDN_CHUNK, GDN_CHUNK), lambda i, j: (0, 0)),
        ],
        out_specs=[
            pl.BlockSpec((tm, tn), lambda i, j: (i, j)),
            pl.BlockSpec((tm, GATE_LANES), lambda i, j: (i, 0)),
        ],
        out_shape=[
            jax.ShapeDtypeStruct((s, n), F32),
            jax.ShapeDtypeStruct((s, GATE_LANES), F32),
        ],
        scratch_shapes=[pltpu.VMEM((tm, d), BF16)],
        compiler_params=_cparams(2),
        name="gdn_zgate",
    )(h, gain, w_in, w_ba, alog_row, dtb_row, tri)


def _gdn_core_kernel(q_ref, k_ref, v_ref, z_ref, gc_ref, gr_ref, on_ref, o_ref,
                     s_scr, mq_scr, nn_scr, oi_scr, dl_scr, *, tb, n_t):
    kh = pl.program_id(0)
    t = pl.program_id(1)
    c_sz = GDN_CHUNK
    grp = GDN_GROUP
    n_in_grp = grp // c_sz
    n_chunks = tb // c_sz
    onorm = on_ref[...]

    @pl.when(t == 0)
    def _():
        s_scr[...] = jnp.zeros_like(s_scr)
        mq_scr[...] = jnp.zeros_like(mq_scr)
        nn_scr[...] = jnp.zeros_like(nn_scr)
        oi_scr[...] = jnp.zeros_like(oi_scr)
        dl_scr[...] = jnp.zeros_like(dl_scr)

    def recurrence_step(state, c):
        rows = slice(c * c_sz, (c + 1) * c_sz)
        for e in range(2):
            idx = e * n_chunks + c
            vcols = slice(e * GDN_HEAD, (e + 1) * GDN_HEAD)
            r = _dot(mq_scr[idx], state[e].astype(BF16))
            o = r[GDN_HEAD:] + oi_scr[idx]
            state[e] = state[e] * dl_scr[idx][0:1, :] - r[:GDN_HEAD] + nn_scr[idx]
            ms = jnp.mean(o * o, axis=-1, keepdims=True)
            o_ref[rows, vcols] = (o * lax.rsqrt(ms + NORM_EPS) * onorm
                                  * z_ref[rows, vcols]).astype(BF16)

    def drain_only():
        state = [s_scr[0], s_scr[1]]
        for c in range(n_chunks):
            recurrence_step(state, c)

    pl.when(t == n_t)(drain_only)
    pl.when(t < n_t)(functools.partial(
        _gdn_core_block, q_ref, k_ref, v_ref, gc_ref, gr_ref, s_scr, mq_scr, nn_scr, oi_scr,
        dl_scr, recurrence_step, kh, tb))


def _gdn_core_block(q_ref, k_ref, v_ref, gc_ref, gr_ref, s_scr, mq_scr, nn_scr, oi_scr,
                    dl_scr, recurrence_step, kh, tb):
    c_sz = GDN_CHUNK
    grp = GDN_GROUP
    n_in_grp = grp // c_sz
    n_chunks = tb // c_sz
    state = [s_scr[0], s_scr[1]]
    pending = list(range(n_chunks))

    def advance(count=1):
        for _ in range(min(count, len(pending))):
            recurrence_step(state, pending.pop(0))

    gates = gc_ref[...]
    lane = lax.broadcasted_iota(jnp.int32, (tb, GATE_LANES), 1)
    ri = lax.broadcasted_iota(jnp.int32, (grp, grp), 0)
    ci = lax.broadcasted_iota(jnp.int32, (grp, grp), 1)
    same_chunk = (ri // c_sz) == (ci // c_sz)
    bd_causal = same_chunk & (ri >= ci)
    bd_strict = same_chunk & (ri > ci)
    bd_mask = jnp.where(same_chunk, 1.0, 0.0).astype(BF16)
    pi = lax.broadcasted_iota(jnp.int32, (c_sz, grp), 0)
    pj = lax.broadcasted_iota(jnp.int32, (c_sz, grp), 1)
    eye_p = jnp.where(pi == (pj % c_sz), 1.0, 0.0).astype(F32)
    nt_dims = (((1,), (1,)), ((), ()))
    tn_dims = (((0,), (0,)), ((), ()))

    def block_diag(packed):
        return jnp.concatenate([packed] * n_in_grp, axis=0) * bd_mask

    per_head = []
    for e in range(2):
        head = 2 * kh + e
        beta_all = jnp.sum(jnp.where(lane == head, gates, 0.0), axis=-1, keepdims=True)
        gc_all = jnp.sum(jnp.where(lane == GDN_V_HEADS + head, gates, 0.0),
                         axis=-1, keepdims=True)
        gr_all = gr_ref[pl.ds(GDN_V_HEADS + head, 1), :]
        per_head.append((beta_all, gc_all, gr_all))

    n_grp = tb // grp
    pairs = [(g, e) for g in range(n_grp) for e in range(2)]
    kgs, qgs, gkk, gqk = [], [], [], []
    for g in range(n_grp):
        rows = slice(g * grp, (g + 1) * grp)
        kg = k_ref[rows, :]
        qg = q_ref[rows, :]
        kb = kg.astype(BF16)
        kgs.append(kg)
        qgs.append(qg)
        gkk.append(lax.dot_general(kb, kb, nt_dims, preferred_element_type=F32))
        gqk.append(lax.dot_general(qg.astype(BF16), kb, nt_dims, preferred_element_type=F32))
    advance()

    bcols, gcols, decays, t_invs, q_pows = [], [], [], [], []
    for g, e in pairs:
        rows = slice(g * grp, (g + 1) * grp)
        beta_all, gc_all, gr_all = per_head[e]
        bcol = beta_all[rows, :]
        gcol = gc_all[rows, :]
        decay = jnp.exp(jnp.minimum(gcol - gr_all[:, rows], 0.0))
        a_nat = jnp.where(bd_strict, bcol * gkk[g] * decay, 0.0)
        p_mat = -(a_nat[0:c_sz] + a_nat[c_sz:2 * c_sz]
                  + a_nat[2 * c_sz:3 * c_sz] + a_nat[3 * c_sz:4 * c_sz])
        bcols.append(bcol)
        gcols.append(gcol)
        decays.append(decay)
        t_invs.append(eye_p + p_mat)
        q_pows.append(p_mat)

    advance()
    for i in range(len(pairs)):
        pb = q_pows[i].astype(BF16)
        q_pows[i] = _dot(pb, block_diag(pb))
    advance()
    for _ in range(int(math.log2(c_sz)) - 2):
        for i in range(len(pairs)):
            tq = _dot(jnp.concatenate([t_invs[i], q_pows[i]], axis=0).astype(BF16),
                      block_diag(q_pows[i].astype(BF16)))
            t_invs[i] = t_invs[i] + tq[:c_sz]
            q_pows[i] = tq[c_sz:]
        advance()
    for i in range(len(pairs)):
        t_invs[i] = t_invs[i] + _dot(t_invs[i].astype(BF16),
                                     block_diag(q_pows[i].astype(BF16)))
    advance()

    egs, wus = [], []
    for i, (g, e) in enumerate(pairs):
        rows = slice(g * grp, (g + 1) * grp)
        vcols = slice(e * GDN_HEAD, (e + 1) * GDN_HEAD)
        eg = jnp.exp(gcols[i])
        rhs = jnp.concatenate([kgs[g] * (bcols[i] * eg), v_ref[rows, vcols] * bcols[i]],
                              axis=1).astype(BF16)
        egs.append(eg)
        wus.append(_dot(block_diag(t_invs[i].astype(BF16)), rhs).astype(BF16))
    advance()
    qws = []
    for i, (g, e) in enumerate(pairs):
        qkm = jnp.where(bd_causal, gqk[g] * decays[i], 0.0).astype(BF16)
        qws.append(_dot(qkm, wus[i]))
    advance()

    new_ops = []
    for g in range(n_grp):
        for m in range(n_in_grp):
            cr = slice(m * c_sz, (m + 1) * c_sz)
            for e in range(2):
                i = 2 * g + e
                gcol = gcols[i]
                glast = gcol[(m + 1) * c_sz - 1:(m + 1) * c_sz, :]
                kd = (kgs[g][cr] * jnp.exp(glast - gcol[cr])).astype(BF16)
                mn = lax.dot_general(kd, wus[i][cr], tn_dims,
                                     preferred_element_type=F32)
                q_eff = (qgs[g][cr] * egs[i][cr] - qws[i][cr, :GDN_HEAD]).astype(BF16)
                new_ops.append((
                    e * n_chunks + g * n_in_grp + m,
                    jnp.concatenate([mn[:, :GDN_HEAD].astype(BF16), q_eff], axis=0),
                    mn[:, GDN_HEAD:],
                    qws[i][cr, GDN_HEAD:],
                    jnp.broadcast_to(jnp.exp(glast), (SUBLANES, GDN_HEAD))))
        advance()

    advance(n_chunks)
    s_scr[0] = state[0]
    s_scr[1] = state[1]
    for idx, mq_v, nn_v, oi_v, dl_v in new_ops:
        mq_scr[idx] = mq_v
        nn_scr[idx] = nn_v
        oi_scr[idx] = oi_v
        dl_scr[idx] = dl_v


def _gdn_core(qkv, zs, gates_c, gates_r, o_norm_row, *, tb):
    s = qkv.shape[0]
    nk = GDN_K_HEADS
    n_t = s // tb
    n_ops = 2 * (tb // GDN_CHUNK)

    def cur(kh, t):
        return jnp.minimum(t, n_t - 1)

    def prev(kh, t):
        return jnp.maximum(t - 1, 0)

    return pl.pallas_call(
        functools.partial(_gdn_core_kernel, tb=tb, n_t=n_t),
        grid=(nk, n_t + 1),
        in_specs=[
            pl.BlockSpec((tb, GDN_HEAD), lambda kh, t: (cur(kh, t), kh)),
            pl.BlockSpec((tb, GDN_HEAD), lambda kh, t: (cur(kh, t), nk + kh)),
            pl.BlockSpec((tb, 2 * GDN_HEAD), lambda kh, t: (cur(kh, t), nk + kh)),
            pl.BlockSpec((tb, 2 * GDN_HEAD), lambda kh, t: (prev(kh, t), kh)),
            pl.BlockSpec((tb, GATE_LANES), lambda kh, t: (cur(kh, t), 0)),
            pl.BlockSpec((GATE_LANES, tb), lambda kh, t: (0, cur(kh, t))),
            pl.BlockSpec((1, GDN_HEAD), lambda kh, t: (0, 0)),
        ],
        out_specs=pl.BlockSpec((tb, 2 * GDN_HEAD), lambda kh, t: (prev(kh, t), kh)),
        out_shape=jax.ShapeDtypeStruct((s, GDN_VAL_DIM), BF16),
        scratch_shapes=[
            pltpu.VMEM((2, GDN_HEAD, GDN_HEAD), F32),
            pltpu.VMEM((n_ops, GDN_HEAD + GDN_CHUNK, GDN_HEAD), BF16),
            pltpu.VMEM((n_ops, GDN_HEAD, GDN_HEAD), F32),
            pltpu.VMEM((n_ops, GDN_CHUNK, GDN_HEAD), F32),
            pltpu.VMEM((n_ops, SUBLANES, GDN_HEAD), F32),
        ],
        compiler_params=_cparams(2),
        name="gdn_core",
    )(qkv, qkv, qkv, zs, gates_c, gates_r, o_norm_row)


def _res_matmul_kernel(h_ref, a_ref, w_ref, o_ref):
    o_ref[...] = h_ref[...] + _dot(a_ref[...], w_ref[...])


def _res_matmul(h, a, w, layer, *, tm, tn):
    s, d = h.shape
    kdim = a.shape[1]
    return pl.pallas_call(
        _res_matmul_kernel,
        grid=(s // tm, d // tn),
        in_specs=[
            pl.BlockSpec((tm, tn), lambda i, j: (i, j)),
            pl.BlockSpec((tm, kdim), lambda i, j: (i, 0)),
            pl.BlockSpec((None, kdim, tn), lambda i, j: (layer, 0, j)),
        ],
        out_specs=pl.BlockSpec((tm, tn), lambda i, j: (i, j)),
        out_shape=jax.ShapeDtypeStruct((s, d), F32),
        compiler_params=_cparams(2),
        name="res_matmul",
    )(h, a, w)


def _norm_matmul_kernel(h_ref, g_ref, w_ref, o_ref, hn_scr):
    @pl.when(pl.program_id(1) == 0)
    def _():
        hn_scr[...] = _rms_rows(h_ref[...], g_ref[...]).astype(BF16)

    o_ref[...] = _dot(hn_scr[...], w_ref[...])


def _norm_matmul(h, gain, w, layer, *, tm, tn):
    s, d = h.shape
    n = w.shape[-1]
    return pl.pallas_call(
        _norm_matmul_kernel,
        grid=(s // tm, n // tn),
        in_specs=[
            pl.BlockSpec((tm, d), lambda i, j: (i, 0)),
            pl.BlockSpec((1, d), lambda i, j: (0, 0)),
            pl.BlockSpec((None, d, tn), lambda i, j: (layer, 0, j)),
        ],
        out_specs=pl.BlockSpec((tm, tn), lambda i, j: (i, j)),
        out_shape=jax.ShapeDtypeStruct((s, n), F32),
        scratch_shapes=[pltpu.VMEM((tm, d), BF16)],
        compiler_params=_cparams(2),
        name="s5_in_proj",
    )(h, gain, w)


def _s5_core_kernel(u_ref, d_ref, toe_ref, wst_ref, wrd_ref, lam_r_ref, lam_i_ref, a_ref,
                    uvec_scr, y_scr, *, n_blocks):
    gpt = S5_GROUPS_PER_TILE
    half = S5_BLOCK // 2
    rc = S5_ROW_CHUNK
    piece = lax.broadcasted_iota(jnp.int32, (rc, LANES), 1) // S5_GROUP_CH
    row = lax.broadcasted_iota(jnp.int32, (n_blocks, LANES), 0)

    def time_rows(n0, step):
        return pl.ds(n0 * S5_BLOCK + step, rc, stride=S5_BLOCK)

    def butterfly_transpose(vs, index, axis, unit):
        vs = list(vs)
        size = 8 * unit
        for dist in (4, 2, 1):
            low = (index // dist) % 2 == 0
            nxt = list(vs)
            for i in range(len(vs)):
                if (i // dist) % 2 == 0:
                    j = i + dist
                    nxt[i] = jnp.where(low, vs[i], pltpu.roll(vs[j], unit * dist, axis))
                    nxt[j] = jnp.where(low, pltpu.roll(vs[i], size - unit * dist, axis), vs[j])
            vs = nxt
        return vs

    def transpose_pieces(vs):
        return butterfly_transpose(vs, piece[:vs[0].shape[0]], 1, S5_GROUP_CH)

    def gather(it, carry):
        n0 = pl.multiple_of(it * rc, rc)
        halves = [transpose_pieces([u_ref[time_rows(n0, hh * half + sp), :]
                                    for sp in range(half)]) for hh in range(2)]
        for g in range(gpt):
            uvec_scr[g, pl.ds(n0, rc), :] = jnp.concatenate(
                [halves[0][g], halves[1][g]], axis=1).astype(BF16)
        return carry

    lax.fori_loop(0, n_blocks // rc, gather, 0, unroll=S5_RELAYOUT_UNROLL)

    y_intra = [_dot(uvec_scr[g], toe_ref[g]) for g in range(gpt)]
    res, ims = [], []
    for k in range(gpt // 2):
        pair = jnp.concatenate([uvec_scr[2 * k], uvec_scr[2 * k + 1]], axis=1)
        st = _dot(pair, wst_ref[k])
        res.append(st[:, :LANES])
        ims.append(st[:, LANES:])
    for r in range(int(math.log2(n_blocks))):
        dist = 1 << r
        for k in range(gpt // 2):
            lr = lam_r_ref[k, r:r + 1, :]
            li = lam_i_ref[k, r:r + 1, :]
            sre = jnp.where(row >= dist, pltpu.roll(res[k], dist, 0), 0.0)
            sim = jnp.where(row >= dist, pltpu.roll(ims[k], dist, 0), 0.0)
            res[k] = res[k] + sre * lr - sim * li
            ims[k] = ims[k] + sre * li + sim * lr
    for k in range(gpt // 2):
        prev = jnp.concatenate([jnp.where(row >= 1, pltpu.roll(res[k], 1, 0), 0.0),
                                jnp.where(row >= 1, pltpu.roll(ims[k], 1, 0), 0.0)], axis=1)
        y_inter = _dot(prev.astype(BF16), wrd_ref[k])
        y_scr[2 * k] = y_intra[2 * k] + y_inter[:, :S5_BLOCK_W]
        y_scr[2 * k + 1] = y_intra[2 * k + 1] + y_inter[:, S5_BLOCK_W:]

    dsk = d_ref[...]

    sub = lax.broadcasted_iota(jnp.int32, (SUBLANES, LANES), 0)

    def scatter(it, carry):
        n0 = pl.multiple_of(it * SUBLANES, SUBLANES)
        for hh in range(2):
            cols = slice(hh * LANES, (hh + 1) * LANES)
            by_time = transpose_pieces([y_scr[g, pl.ds(n0, SUBLANES), cols] for g in range(gpt)])
            by_block = butterfly_transpose(by_time, sub, 0, 1)
            for i in range(SUBLANES):
                rows = pl.ds(pl.multiple_of((n0 + i) * S5_BLOCK + hh * half, half), half)
                a_ref[rows, :] = _gelu_tanh(by_block[i] + dsk * u_ref[rows, :])
        return carry

    lax.fori_loop(0, n_blocks // SUBLANES, scatter, 0, unroll=S5_RELAYOUT_UNROLL)


def _s5_core(u, d_row, toe, w_state, w_read, lam_r, lam_i):
    s, d = u.shape
    n_blocks = s // S5_BLOCK
    gpt = S5_GROUPS_PER_TILE
    levels = lam_r.shape[1]
    return pl.pallas_call(
        functools.partial(_s5_core_kernel, n_blocks=n_blocks),
        grid=(d // LANES,),
        in_specs=[
            pl.BlockSpec((s, LANES), lambda i: (0, i)),
            pl.BlockSpec((1, LANES), lambda i: (0, i)),
            pl.BlockSpec((gpt, S5_BLOCK_W, S5_BLOCK_W), lambda i: (i, 0, 0)),
            pl.BlockSpec((gpt // 2, 2 * S5_BLOCK_W, 4 * S5_STATE), lambda i: (i, 0, 0)),
            pl.BlockSpec((gpt // 2, 4 * S5_STATE, 2 * S5_BLOCK_W), lambda i: (i, 0, 0)),
            pl.BlockSpec((gpt // 2, levels, 2 * S5_STATE), lambda i: (i, 0, 0)),
            pl.BlockSpec((gpt // 2, levels, 2 * S5_STATE), lambda i: (i, 0, 0)),
        ],
        out_specs=pl.BlockSpec((s, LANES), lambda i: (0, i)),
        out_shape=jax.ShapeDtypeStruct((s, d), F32),
        scratch_shapes=[
            pltpu.VMEM((gpt, n_blocks, S5_BLOCK_W), BF16),
            pltpu.VMEM((gpt, n_blocks, S5_BLOCK_W), F32),
        ],
        compiler_params=_cparams(1),
        name="s5_core",
    )(u, d_row, toe, w_state, w_read, lam_r, lam_i)


def _s5_operators(lam_re, lam_im, log_step, b_re, b_im, c_re, c_im, n_blocks):
    g = lam_re.shape[0]
    lam = lax.complex(lam_re, lam_im)
    dt = jnp.exp(log_step)[:, None]
    lam_dt = lam * dt
    lam_bar = jnp.exp(lam_dt)
    b_bar = lax.complex(b_re, b_im) * ((lam_bar - 1.0) / lam)[..., None]
    cc = lax.complex(c_re, c_im)
    taus = jnp.arange(S5_BLOCK + 1, dtype=F32)
    powers = jnp.exp(lam_dt[:, None, :] * taus[None, :, None])
    kern = jnp.einsum('gcp,gtp,gpd->gtcd', cc, powers[:, :S5_BLOCK], b_bar).real
    s_idx = jnp.arange(S5_BLOCK)
    lag = s_idx[None, :] - s_idx[:, None]
    toe = jnp.where((lag >= 0)[None, :, :, None, None],
                    kern[:, jnp.clip(lag, 0, S5_BLOCK - 1)], 0.0)
    toe = toe.transpose(0, 1, 4, 2, 3).reshape(g, S5_BLOCK_W, S5_BLOCK_W)
    bst = powers[:, S5_BLOCK - 1 - s_idx, :, None] * b_bar[:, None]
    bst = bst.transpose(0, 1, 3, 2).reshape(g, S5_BLOCK_W, S5_STATE)
    zb = jnp.zeros((g // 2, S5_BLOCK_W, S5_STATE), F32)
    w_state = jnp.concatenate([
        jnp.concatenate([bst.real[0::2], zb, bst.imag[0::2], zb], axis=-1),
        jnp.concatenate([zb, bst.real[1::2], zb, bst.imag[1::2]], axis=-1),
    ], axis=1).astype(BF16)
    cpow = cc[:, None] * powers[:, 1:S5_BLOCK + 1, None, :]
    cpow = cpow.transpose(0, 3, 1, 2).reshape(g, S5_STATE, S5_BLOCK_W)
    zc = jnp.zeros((g // 2, S5_STATE, S5_BLOCK_W), F32)
    w_read = jnp.concatenate([
        jnp.concatenate([cpow.real[0::2], zc], axis=-1),
        jnp.concatenate([zc, cpow.real[1::2]], axis=-1),
        jnp.concatenate([-cpow.imag[0::2], zc], axis=-1),
        jnp.concatenate([zc, -cpow.imag[1::2]], axis=-1),
    ], axis=1).astype(BF16)
    levels = int(math.log2(n_blocks))
    steps = (S5_BLOCK * (2.0 ** jnp.arange(levels, dtype=F32)))
    lam_pow = jnp.exp(lam_dt[:, None, :] * steps[None, :, None])
    lam_r = jnp.concatenate([lam_pow.real[0::2], lam_pow.real[1::2]], axis=-1)
    lam_i = jnp.concatenate([lam_pow.imag[0::2], lam_pow.imag[1::2]], axis=-1)
    return toe.astype(BF16), w_state, w_read, lam_r, lam_i


def _s5_out_kernel(h_ref, a_ref, wa_ref, wb_ref, o_ref, a_scr):
    @pl.when(pl.program_id(1) == 0)
    def _():
        a_scr[...] = a_ref[...].astype(BF16)

    a = a_scr[...]
    o_ref[...] = h_ref[...] + _dot(a, wa_ref[...]) * _sigmoid(_dot(a, wb_ref[...]))


def _s5_out(h, a, w_out, layer, *, tm, tn):
    s, d = h.shape
    nb = d // tn
    return pl.pallas_call(
        _s5_out_kernel,
        grid=(s // tm, nb),
        in_specs=[
            pl.BlockSpec((tm, tn), lambda i, j: (i, j)),
            pl.BlockSpec((tm, d), lambda i, j: (i, 0)),
            pl.BlockSpec((None, d, tn), lambda i, j: (layer, 0, j)),
            pl.BlockSpec((None, d, tn), lambda i, j: (layer, 0, nb + j)),
        ],
        out_specs=pl.BlockSpec((tm, tn), lambda i, j: (i, j)),
        out_shape=jax.ShapeDtypeStruct((s, d), F32),
        scratch_shapes=[pltpu.VMEM((tm, d), BF16)],
        compiler_params=_cparams(2),
        name="s5_out",
    )(h, a, w_out, w_out)


def _mlp_kernel(h_ref, g_ref, wu_ref, wd_ref, o_ref, hn_scr):
    @pl.when(pl.program_id(1) == 0)
    def _():
        h = h_ref[...]
        hn_scr[...] = _rms_rows(h, g_ref[...]).astype(BF16)
        o_ref[...] = h

    a = jnp.maximum(_dot(hn_scr[...], wu_ref[...]), 0.0)
    o_ref[...] += _dot((a * a).astype(BF16), wd_ref[...])


def _mlp(h, gain, w_up, w_down, layer, *, tm, tf):
    s, d = h.shape
    ff = w_up.shape[-1]
    return pl.pallas_call(
        _mlp_kernel,
        grid=(s // tm, ff // tf),
        in_specs=[
            pl.BlockSpec((tm, d), lambda i, k: (i, 0)),
            pl.BlockSpec((1, d), lambda i, k: (0, 0)),
            pl.BlockSpec((None, d, tf), lambda i, k: (layer, 0, k)),
            pl.BlockSpec((None, tf, d), lambda i, k: (layer, k, 0)),
        ],
        out_specs=pl.BlockSpec((tm, d), lambda i, k: (i, 0)),
        out_shape=jax.ShapeDtypeStruct((s, d), F32),
        scratch_shapes=[pltpu.VMEM((tm, d), BF16)],
        compiler_params=_cparams(2),
        name="mlp",
    )(h, gain, w_up, w_down)


def _ple_kernel(h_ref, g_ref, p_ref, wg_ref, wp_ref, o_ref, hn_scr, pb_scr, *, tn):
    j = pl.program_id(1)

    @pl.when(j == 0)
    def _():
        hn_scr[...] = _rms_rows(h_ref[...], g_ref[...]).astype(BF16)
        pb_scr[...] = p_ref[...].astype(BF16)

    gate = _sigmoid(_dot(hn_scr[...], wg_ref[...]))
    emb = _dot(pb_scr[...], wp_ref[...])
    cols = pl.ds(pl.multiple_of(j * tn, tn), tn)
    o_ref[...] = h_ref[:, cols] + gate * emb


def _ple(h, gain, p, layer, batch, w_gate, w_proj, *, tm, tn):
    s, d = h.shape
    pd = p.shape[-1]
    return pl.pallas_call(
        functools.partial(_ple_kernel, tn=tn),
        grid=(s // tm, d // tn),
        in_specs=[
            pl.BlockSpec((tm, d), lambda i, j: (i, 0)),
            pl.BlockSpec((1, d), lambda i, j: (0, 0)),
            pl.BlockSpec((None, None, tm, pd), lambda i, j: (layer, batch, i, 0)),
            pl.BlockSpec((None, d, tn), lambda i, j: (layer, 0, j)),
            pl.BlockSpec((None, pd, tn), lambda i, j: (layer, 0, j)),
        ],
        out_specs=pl.BlockSpec((tm, tn), lambda i, j: (i, j)),
        out_shape=jax.ShapeDtypeStruct((s, d), F32),
        scratch_shapes=[pltpu.VMEM((tm, d), BF16), pltpu.VMEM((tm, pd), BF16)],
        compiler_params=_cparams(2),
        name="ple",
    )(h, gain, p, w_gate, w_proj)


def _final_norm_kernel(h_ref, g_ref, o_ref):
    o_ref[...] = _rms_rows(h_ref[...], g_ref[...])


def _final_norm(h, gain, *, tm):
    s, d = h.shape
    return pl.pallas_call(
        _final_norm_kernel,
        grid=(s // tm,),
        in_specs=[pl.BlockSpec((tm, d), lambda i: (i, 0)),
                  pl.BlockSpec((1, d), lambda i: (0, 0))],
        out_specs=pl.BlockSpec((tm, d), lambda i: (i, 0)),
        out_shape=jax.ShapeDtypeStruct((s, d), F32),
        compiler_params=_cparams(1),
        name="final_norm",
    )(h, gain)


def _tile(n, want):
    t = min(n, want)
    while n % t:
        t //= 2
    return t


def _gdn_layer(h, gain, w_in, w_in_f32, layer, conv_w, a_log, dt_bias, o_norm, w_out):
    s, d = h.shape
    tm = _tile(s, 1024)
    pad_v = GATE_LANES - 2 * GDN_V_HEADS
    w_ba = jnp.pad(w_in_f32[layer, :, GDN_CONV_DIM + GDN_VAL_DIM:].astype(BF16),
                   ((0, 0), (0, pad_v)))
    alog_row = jnp.concatenate([jnp.zeros((GDN_V_HEADS,), F32), a_log,
                                jnp.zeros((pad_v,), F32)])[None, :]
    dtb_row = jnp.concatenate([jnp.zeros((GDN_V_HEADS,), F32), dt_bias,
                               jnp.zeros((pad_v,), F32)])[None, :]
    qkv = _gdn_qkv(h, gain, w_in, layer, conv_w, tm=tm, tn=512)
    zs, gates_c = _gdn_zgate(h, gain, w_in, layer, w_ba, alog_row, dtb_row, tm=tm, tn=512)
    gates_r = gates_c.T
    on = _gdn_core(qkv, zs, gates_c, gates_r, o_norm[None, :], tb=_tile(s, 512))
    return _res_matmul(h, on, w_out, layer, tm=tm, tn=512)


def _s5_layer(h, gain, w_in, layer, lam_re, lam_im, log_step, b_re, b_im, c_re, c_im, d_skip,
              w_out):
    s, d = h.shape
    u = _norm_matmul(h, gain, w_in, layer, tm=_tile(s, 1024), tn=512)
    operators = _s5_operators(lam_re, lam_im, log_step, b_re, b_im, c_re, c_im, s // S5_BLOCK)
    a = _s5_core(u, d_skip[None, :], *operators)
    return _s5_out(h, a, w_out, layer, tm=_tile(s, 512), tn=512)


def kernel(x, p, norm_mix, norm_mlp, norm_ple, norm_final, gdn_w_in, gdn_conv_w, gdn_a_log,
           gdn_dt_bias, gdn_o_norm, gdn_w_out, s5_w_in, s5_lam_re, s5_lam_im, s5_log_step,
           s5_b_re, s5_b_im, s5_c_re, s5_c_im, s5_d, s5_w_out, mlp_w_up, mlp_w_down,
           ple_w_proj, ple_w_gate):
    bsz, s, d = x.shape
    depth = norm_mix.shape[0]
    gdn_w_in_b, gdn_w_out_b = gdn_w_in.astype(BF16), gdn_w_out.astype(BF16)
    s5_w_in_b, s5_w_out_b = s5_w_in.astype(BF16), s5_w_out.astype(BF16)
    mlp_w_up_b, mlp_w_down_b = mlp_w_up.astype(BF16), mlp_w_down.astype(BF16)
    ple_w_gate_b, ple_w_proj_b = ple_w_gate.astype(BF16), ple_w_proj.astype(BF16)
    outs = []
    for b in range(bsz):
        h = x[b]
        for i in range(depth):
            j = i // 2
            if i % 2 == 0:
                h = _gdn_layer(h, norm_mix[i][None, :], gdn_w_in_b, gdn_w_in, j, gdn_conv_w[j],
                               gdn_a_log[j], gdn_dt_bias[j], gdn_o_norm[j], gdn_w_out_b)
            else:
                h = _s5_layer(h, norm_mix[i][None, :], s5_w_in_b, j, s5_lam_re[j], s5_lam_im[j],
                              s5_log_step[j], s5_b_re[j], s5_b_im[j], s5_c_re[j], s5_c_im[j],
                              s5_d[j], s5_w_out_b)
            h = _mlp(h, norm_mlp[i][None, :], mlp_w_up_b, mlp_w_down_b, i,
                     tm=_tile(s, 1024), tf=512)
            h = _ple(h, norm_ple[i][None, :], p, i, b, ple_w_gate_b, ple_w_proj_b,
                     tm=_tile(s, 512), tn=512)
        outs.append(_final_norm(h, norm_final[None, :], tm=_tile(s, 512)))
    return jnp.stack(outs, axis=0)
```

```python
import functools
import math

import jax
import jax.numpy as jnp
from jax import lax
from jax.experimental import pallas as pl
from jax.experimental.pallas import tpu as pltpu

F32 = jnp.float32
BF16 = jnp.bfloat16

NORM_EPS = 1e-6
L2_EPS = 1e-6

GDN_K_HEADS = 16
GDN_V_HEADS = 32
GDN_HEAD = 128
GDN_CONV = 4
GDN_KEY_DIM = GDN_K_HEADS * GDN_HEAD
GDN_VAL_DIM = GDN_V_HEADS * GDN_HEAD
GDN_CONV_DIM = 2 * GDN_KEY_DIM + GDN_VAL_DIM
GDN_CHUNK = 64
GDN_GROUP = 4 * GDN_CHUNK
QKV_ROW_SUBBLOCK = 256
GATE_LANES = 128

S5_GROUP_CH = 16
S5_STATE = 64
S5_BLOCK = 16
S5_BLOCK_W = S5_BLOCK * S5_GROUP_CH

SUBLANES = 8
LANES = 128
S5_GROUPS_PER_TILE = LANES // S5_GROUP_CH
S5_ROW_CHUNK = 16
S5_RELAYOUT_UNROLL = 4
VMEM_LIMIT_BYTES = 56 * 1024 * 1024


def _cparams(n_axes):
    return pltpu.CompilerParams(
        dimension_semantics=("arbitrary",) * n_axes,
        vmem_limit_bytes=VMEM_LIMIT_BYTES)


def _rms_rows(x, gain):
    ms = jnp.mean(x * x, axis=-1, keepdims=True)
    return x * lax.rsqrt(ms + NORM_EPS) * gain


def _sigmoid(x):
    return 1.0 / (1.0 + jnp.exp(-x))


def _softplus(x):
    return jnp.maximum(x, 0.0) + jnp.log(1.0 + jnp.exp(-jnp.abs(x)))


def _gelu_tanh(x):
    c = math.sqrt(2.0 / math.pi)
    return 0.5 * x * (1.0 + jnp.tanh(c * (x + 0.044715 * (x * x * x))))


def _dot(a, b):
    return jnp.dot(a, b, preferred_element_type=F32)


def _gdn_qkv_kernel(h_ref, g_ref, w_ref, cw_ref, o_ref, hn_scr, tail_scr, buf_scr,
                    *, tm, tn, rb, n_qk_blocks):
    i = pl.program_id(0)
    j = pl.program_id(1)

    @pl.when(j == 0)
    def _():
        hn_scr[...] = _rms_rows(h_ref[...], g_ref[...]).astype(BF16)

    @pl.when(i == 0)
    def _():
        tail_scr[j] = jnp.zeros((SUBLANES, tn), F32)

    buf_scr[0:SUBLANES, :] = tail_scr[j]
    cw = cw_ref[...]

    def row_blocks(l2_norm):
        if l2_norm:
            scale = jnp.where(j < n_qk_blocks // 2, GDN_HEAD ** -0.5, 1.0).astype(F32)
        for r in range(tm // rb):
            rows = slice(r * rb, (r + 1) * rb)
            acc = _dot(hn_scr[rows, :], w_ref[...])
            buf_scr[SUBLANES + r * rb:SUBLANES + (r + 1) * rb, :] = acc
            y = acc * cw[3:4, :]
            for tap in range(GDN_CONV - 1):
                shift = GDN_CONV - 1 - tap
                y = y + buf_scr[pl.ds(SUBLANES + r * rb - shift, rb), :] * cw[tap:tap + 1, :]
            y = y * _sigmoid(y)
            if not l2_norm:
                o_ref[rows, :] = y
                continue
            for hh in range(tn // GDN_HEAD):
                sl = slice(hh * GDN_HEAD, (hh + 1) * GDN_HEAD)
                xs = y[:, sl]
                inv = lax.rsqrt(jnp.sum(xs * xs, axis=-1, keepdims=True) + L2_EPS)
                o_ref[rows, sl] = xs * (inv * scale)
        tail_scr[j] = buf_scr[tm:tm + SUBLANES, :]

    pl.when(j < n_qk_blocks)(functools.partial(row_blocks, True))
    pl.when(j >= n_qk_blocks)(functools.partial(row_blocks, False))


def _gdn_qkv(h, gain, w_in, layer, conv_w, *, tm, tn):
    s, d = h.shape
    n = conv_w.shape[1]
    kern = functools.partial(_gdn_qkv_kernel, tm=tm, tn=tn, rb=_tile(tm, QKV_ROW_SUBBLOCK),
                             n_qk_blocks=2 * GDN_KEY_DIM // tn)
    return pl.pallas_call(
        kern,
        grid=(s // tm, n // tn),
        in_specs=[
            pl.BlockSpec((tm, d), lambda i, j: (i, 0)),
            pl.BlockSpec((1, d), lambda i, j: (0, 0)),
            pl.BlockSpec((None, d, tn), lambda i, j: (layer, 0, j)),
            pl.BlockSpec((GDN_CONV, tn), lambda i, j: (0, j)),
        ],
        out_specs=pl.BlockSpec((tm, tn), lambda i, j: (i, j)),
        out_shape=jax.ShapeDtypeStruct((s, n), F32),
        scratch_shapes=[
            pltpu.VMEM((tm, d), BF16),
            pltpu.VMEM((n // tn, SUBLANES, tn), F32),
            pltpu.VMEM((tm + SUBLANES, tn), F32),
        ],
        compiler_params=_cparams(2),
        name="gdn_qkv",
    )(h, gain, w_in, conv_w)


def _gdn_zgate_kernel(h_ref, g_ref, wz_ref, wba_ref, alog_ref, dtb_ref, tri_ref,
                      z_ref, gate_ref, hn_scr, *, tm):
    j = pl.program_id(1)

    @pl.when(j == 0)
    def _():
        hn = _rms_rows(h_ref[...], g_ref[...]).astype(BF16)
        hn_scr[...] = hn
        ba = _dot(hn, wba_ref[...])
        beta = _sigmoid(ba)
        g = -jnp.exp(alog_ref[...]) * _softplus(ba + dtb_ref[...])
        lane = lax.broadcasted_iota(jnp.int32, (GDN_CHUNK, GATE_LANES), 1)
        tri = tri_ref[...]
        for c in range(tm // GDN_CHUNK):
            rows = slice(c * GDN_CHUNK, (c + 1) * GDN_CHUNK)
            gc = jnp.dot(tri, g[rows, :], preferred_element_type=F32,
                         precision=lax.Precision.HIGHEST)
            gate_ref[rows, :] = jnp.where(lane < GDN_V_HEADS, beta[rows, :], gc)

    z = _dot(hn_scr[...], wz_ref[...])
    z_ref[...] = z * _sigmoid(z)


def _gdn_zgate(h, gain, w_in, layer, w_ba, alog_row, dtb_row, *, tm, tn):
    s, d = h.shape
    n = GDN_VAL_DIM
    z_off = GDN_CONV_DIM // tn
    tri = jnp.tril(jnp.ones((GDN_CHUNK, GDN_CHUNK), F32))
    return pl.pallas_call(
        functools.partial(_gdn_zgate_kernel, tm=tm),
        grid=(s // tm, n // tn),
        in_specs=[
            pl.BlockSpec((tm, d), lambda i, j: (i, 0)),
            pl.BlockSpec((1, d), lambda i, j: (0, 0)),
            pl.BlockSpec((None, d, tn), lambda i, j: (layer, 0, z_off + j)),
            pl.BlockSpec((d, GATE_LANES), lambda i, j: (0, 0)),
            pl.BlockSpec((1, GATE_LANES), lambda i, j: (0, 0)),
            pl.BlockSpec((1, GATE_LANES), lambda i, j: (0, 0)),
            pl.BlockSpec((GDN_CHUNK, GDN_CHUNK), lambda i, j: (0, 0)),
        ],
        out_specs=[
            pl.BlockSpec((tm, tn), lambda i, j: (i, j)),
            pl.BlockSpec((tm, GATE_LANES), lambda i, j: (i, 0)),
        ],
        out_shape=[
            jax.ShapeDtypeStruct((s, n), F32),
            jax.ShapeDtypeStruct((s, GATE_LANES), F32),
        ],
        scratch_shapes=[pltpu.VMEM((tm, d), BF16)],
        compiler_params=_cparams(2),
        name="gdn_zgate",
    )(h, gain, w_in, w_ba, alog_row, dtb_row, tri)


def _gdn_core_kernel(q_ref, k_ref, v_ref, z_ref, gc_ref, gr_ref, on_ref, o_ref,
                     s_scr, mq_scr, nn_scr, oi_scr, dl_scr, *, tb, n_t):
    kh = pl.program_id(0)
    t = pl.program_id(1)
    c_sz = GDN_CHUNK
    grp = GDN_GROUP
    n_in_grp = grp // c_sz
    n_chunks = tb // c_sz
    onorm = on_ref[...]

    @pl.when(t == 0)
    def _():
        s_scr[...] = jnp.zeros_like(s_scr)
        mq_scr[...] = jnp.zeros_like(mq_scr)
        nn_scr[...] = jnp.zeros_like(nn_scr)
        oi_scr[...] = jnp.zeros_like(oi_scr)
        dl_scr[...] = jnp.zeros_like(dl_scr)

    def recurrence_step(state, c):
        rows = slice(c * c_sz, (c + 1) * c_sz)
        for e in range(2):
            idx = e * n_chunks + c
            vcols = slice(e * GDN_HEAD, (e + 1) * GDN_HEAD)
            r = _dot(mq_scr[idx], state[e].astype(BF16))
            o = r[GDN_HEAD:] + oi_scr[idx]
            state[e] = state[e] * dl_scr[idx][0:1, :] - r[:GDN_HEAD] + nn_scr[idx]
            ms = jnp.mean(o * o, axis=-1, keepdims=True)
            o_ref[rows, vcols] = (o * lax.rsqrt(ms + NORM_EPS) * onorm
                                  * z_ref[rows, vcols]).astype(BF16)

    def drain_only():
        state = [s_scr[0], s_scr[1]]
        for c in range(n_chunks):
            recurrence_step(state, c)

    pl.when(t == n_t)(drain_only)
    pl.when(t < n_t)(functools.partial(
        _gdn_core_block, q_ref, k_ref, v_ref, gc_ref, gr_ref, s_scr, mq_scr, nn_scr, oi_scr,
        dl_scr, recurrence_step, kh, tb))


def _gdn_core_block(q_ref, k_ref, v_ref, gc_ref, gr_ref, s_scr, mq_scr, nn_scr, oi_scr,
                    dl_scr, recurrence_step, kh, tb):
    c_sz = GDN_CHUNK
    grp = GDN_GROUP
    n_in_grp = grp // c_sz
    n_chunks = tb // c_sz
    state = [s_scr[0], s_scr[1]]
    pending = list(range(n_chunks))

    def advance(count=1):
        for _ in range(min(count, len(pending))):
            recurrence_step(state, pending.pop(0))

    gates = gc_ref[...]
    lane = lax.broadcasted_iota(jnp.int32, (tb, GATE_LANES), 1)
    ri = lax.broadcasted_iota(jnp.int32, (grp, grp), 0)
    ci = lax.broadcasted_iota(jnp.int32, (grp, grp), 1)
    same_chunk = (ri // c_sz) == (ci // c_sz)
    bd_causal = same_chunk & (ri >= ci)
    bd_strict = same_chunk & (ri > ci)
    bd_mask = jnp.where(same_chunk, 1.0, 0.0).astype(BF16)
    pi = lax.broadcasted_iota(jnp.int32, (c_sz, grp), 0)
    pj = lax.broadcasted_iota(jnp.int32, (c_sz, grp), 1)
    eye_p = jnp.where(pi == (pj % c_sz), 1.0, 0.0).astype(F32)
    nt_dims = (((1,), (1,)), ((), ()))
    tn_dims = (((0,), (0,)), ((), ()))

    def block_diag(packed):
        return jnp.concatenate([packed] * n_in_grp, axis=0) * bd_mask

    per_head = []
    for e in range(2):
        head = 2 * kh + e
        beta_all = jnp.sum(jnp.where(lane == head, gates, 0.0), axis=-1, keepdims=True)
        gc_all = jnp.sum(jnp.where(lane == GDN_V_HEADS + head, gates, 0.0),
                         axis=-1, keepdims=True)
        gr_all = gr_ref[pl.ds(GDN_V_HEADS + head, 1), :]
        per_head.append((beta_all, gc_all, gr_all))

    n_grp = tb // grp
    pairs = [(g, e) for g in range(n_grp) for e in range(2)]
    kgs, qgs, gkk, gqk = [], [], [], []
    for g in range(n_grp):
        rows = slice(g * grp, (g + 1) * grp)
        kg = k_ref[rows, :]
        qg = q_ref[rows, :]
        kb = kg.astype(BF16)
        kgs.append(kg)
        qgs.append(qg)
        gkk.append(lax.dot_general(kb, kb, nt_dims, preferred_element_type=F32))
        gqk.append(lax.dot_general(qg.astype(BF16), kb, nt_dims, preferred_element_type=F32))
    advance()

    bcols, gcols, decays, t_invs, q_pows = [], [], [], [], []
    for g, e in pairs:
        rows = slice(g * grp, (g + 1) * grp)
        beta_all, gc_all, gr_all = per_head[e]
        bcol = beta_all[rows, :]
        gcol = gc_all[rows, :]
        decay = jnp.exp(jnp.minimum(gcol - gr_all[:, rows], 0.0))
        a_nat = jnp.where(bd_strict, bcol * gkk[g] * decay, 0.0)
        p_mat = -(a_nat[0:c_sz] + a_nat[c_sz:2 * c_sz]
                  + a_nat[2 * c_sz:3 * c_sz] + a_nat[3 * c_sz:4 * c_sz])
        bcols.append(bcol)
        gcols.append(gcol)
        decays.append(decay)
        t_invs.append(eye_p + p_mat)
        q_pows.append(p_mat)

    advance()
    for i in range(len(pairs)):
        pb = q_pows[i].astype(BF16)
        q_pows[i] = _dot(pb, block_diag(pb))
    advance()
    for _ in range(int(math.log2(c_sz)) - 2):
        for i in range(len(pairs)):
            tq = _dot(jnp.concatenate([t_invs[i], q_pows[i]], axis=0).astype(BF16),
                      block_diag(q_pows[i].astype(BF16)))
            t_invs[i] = t_invs[i] + tq[:c_sz]
            q_pows[i] = tq[c_sz:]
        advance()
    for i in range(len(pairs)):
        t_invs[i] = t_invs[i] + _dot(t_invs[i].astype(BF16),
                                     block_diag(q_pows[i].astype(BF16)))
    advance()

    egs, wus = [], []
    for i, (g, e) in enumerate(pairs):
        rows = slice(g * grp, (g + 1) * grp)
        vcols = slice(e * GDN_HEAD, (e + 1) * GDN_HEAD)
        eg = jnp.exp(gcols[i])
        rhs = jnp.concatenate([kgs[g] * (bcols[i] * eg), v_ref[rows, vcols] * bcols[i]],
                              axis=1).astype(BF16)
        egs.append(eg)
        wus.append(_dot(block_diag(t_invs[i].astype(BF16)), rhs).astype(BF16))
    advance()
    qws = []
    for i, (g, e) in enumerate(pairs):
        qkm = jnp.where(bd_causal, gqk[g] * decays[i], 0.0).astype(BF16)
        qws.append(_dot(qkm, wus[i]))
    advance()

    new_ops = []
    for g in range(n_grp):
        for m in range(n_in_grp):
            cr = slice(m * c_sz, (m + 1) * c_sz)
            for e in range(2):
                i = 2 * g + e
                gcol = gcols[i]
                glast = gcol[(m + 1) * c_sz - 1:(m + 1) * c_sz, :]
                kd = (kgs[g][cr] * jnp.exp(glast - gcol[cr])).astype(BF16)
                mn = lax.dot_general(kd, wus[i][cr], tn_dims,
                                     preferred_element_type=F32)
                q_eff = (qgs[g][cr] * egs[i][cr] - qws[i][cr, :GDN_HEAD]).astype(BF16)
                new_ops.append((
                    e * n_chunks + g * n_in_grp + m,
                    jnp.concatenate([mn[:, :GDN_HEAD].astype(BF16), q_eff], axis=0),
                    mn[:, GDN_HEAD:],
                    qws[i][cr, GDN_HEAD:],
                    jnp.broadcast_to(jnp.exp(glast), (SUBLANES, GDN_HEAD))))
        advance()

    advance(n_chunks)
    s_scr[0] = state[0]
    s_scr[1] = state[1]
    for idx, mq_v, nn_v, oi_v, dl_v in new_ops:
        mq_scr[idx] = mq_v
        nn_scr[idx] = nn_v
        oi_scr[idx] = oi_v
        dl_scr[idx] = dl_v


def _gdn_core(qkv, zs, gates_c, gates_r, o_norm_row, *, tb):
    s = qkv.shape[0]
    nk = GDN_K_HEADS
    n_t = s // tb
    n_ops = 2 * (tb // GDN_CHUNK)

    def cur(kh, t):
        return jnp.minimum(t, n_t - 1)

    def prev(kh, t):
        return jnp.maximum(t - 1, 0)

    return pl.pallas_call(
        functools.partial(_gdn_core_kernel, tb=tb, n_t=n_t),
        grid=(nk, n_t + 1),
        in_specs=[
            pl.BlockSpec((tb, GDN_HEAD), lambda kh, t: (cur(kh, t), kh)),
            pl.BlockSpec((tb, GDN_HEAD), lambda kh, t: (cur(kh, t), nk + kh)),
            pl.BlockSpec((tb, 2 * GDN_HEAD), lambda kh, t: (cur(kh, t), nk + kh)),
            pl.BlockSpec((tb, 2 * GDN_HEAD), lambda kh, t: (prev(kh, t), kh)),
            pl.BlockSpec((tb, GATE_LANES), lambda kh, t: (cur(kh, t), 0)),
            pl.BlockSpec((GATE_LANES, tb), lambda kh, t: (0, cur(kh, t))),
            pl.BlockSpec((1, GDN_HEAD), lambda kh, t: (0, 0)),
        ],
        out_specs=pl.BlockSpec((tb, 2 * GDN_HEAD), lambda kh, t: (prev(kh, t), kh)),
        out_shape=jax.ShapeDtypeStruct((s, GDN_VAL_DIM), BF16),
        scratch_shapes=[
            pltpu.VMEM((2, GDN_HEAD, GDN_HEAD), F32),
            pltpu.VMEM((n_ops, GDN_HEAD + GDN_CHUNK, GDN_HEAD), BF16),
            pltpu.VMEM((n_ops, GDN_HEAD, GDN_HEAD), F32),
            pltpu.VMEM((n_ops, GDN_CHUNK, GDN_HEAD), F32),
            pltpu.VMEM((n_ops, SUBLANES, GDN_HEAD), F32),
        ],
        compiler_params=_cparams(2),
        name="gdn_core",
    )(qkv, qkv, qkv, zs, gates_c, gates_r, o_norm_row)


def _res_matmul_kernel(h_ref, a_ref, w_ref, o_ref):
    o_ref[...] = h_ref[...] + _dot(a_ref[...], w_ref[...])


def _res_matmul(h, a, w, layer, *, tm, tn):
    s, d = h.shape
    kdim = a.shape[1]
    return pl.pallas_call(
        _res_matmul_kernel,
        grid=(s // tm, d // tn),
        in_specs=[
            pl.BlockSpec((tm, tn), lambda i, j: (i, j)),
            pl.BlockSpec((tm, kdim), lambda i, j: (i, 0)),
            pl.BlockSpec((None, kdim, tn), lambda i, j: (layer, 0, j)),
        ],
        out_specs=pl.BlockSpec((tm, tn), lambda i, j: (i, j)),
        out_shape=jax.ShapeDtypeStruct((s, d), F32),
        compiler_params=_cparams(2),
        name="res_matmul",
    )(h, a, w)


def _norm_matmul_kernel(h_ref, g_ref, w_ref, o_ref, hn_scr):
    @pl.when(pl.program_id(1) == 0)
    def _():
        hn_scr[...] = _rms_rows(h_ref[...], g_ref[...]).astype(BF16)

    o_ref[...] = _dot(hn_scr[...], w_ref[...])


def _norm_matmul(h, gain, w, layer, *, tm, tn):
    s, d = h.shape
    n = w.shape[-1]
    return pl.pallas_call(
        _norm_matmul_kernel,
        grid=(s // tm, n // tn),
        in_specs=[
            pl.BlockSpec((tm, d), lambda i, j: (i, 0)),
            pl.BlockSpec((1, d), lambda i, j: (0, 0)),
            pl.BlockSpec((None, d, tn), lambda i, j: (layer, 0, j)),
        ],
        out_specs=pl.BlockSpec((tm, tn), lambda i, j: (i, j)),
        out_shape=jax.ShapeDtypeStruct((s, n), F32),
        scratch_shapes=[pltpu.VMEM((tm, d), BF16)],
        compiler_params=_cparams(2),
        name="s5_in_proj",
    )(h, gain, w)


def _s5_core_kernel(u_ref, d_ref, toe_ref, wst_ref, wrd_ref, lam_r_ref, lam_i_ref, a_ref,
                    uvec_scr, y_scr, *, n_blocks):
    gpt = S5_GROUPS_PER_TILE
    half = S5_BLOCK // 2
    rc = S5_ROW_CHUNK
    piece = lax.broadcasted_iota(jnp.int32, (rc, LANES), 1) // S5_GROUP_CH
    row = lax.broadcasted_iota(jnp.int32, (n_blocks, LANES), 0)

    def time_rows(n0, step):
        return pl.ds(n0 * S5_BLOCK + step, rc, stride=S5_BLOCK)

    def butterfly_transpose(vs, index, axis, unit):
        vs = list(vs)
        size = 8 * unit
        for dist in (4, 2, 1):
            low = (index // dist) % 2 == 0
            nxt = list(vs)
            for i in range(len(vs)):
                if (i // dist) % 2 == 0:
                    j = i + dist
                    nxt[i] = jnp.where(low, vs[i], pltpu.roll(vs[j], unit * dist, axis))
                    nxt[j] = jnp.where(low, pltpu.roll(vs[i], size - unit * dist, axis), vs[j])
            vs = nxt
        return vs

    def transpose_pieces(vs):
        return butterfly_transpose(vs, piece[:vs[0].shape[0]], 1, S5_GROUP_CH)

    def gather(it, carry):
        n0 = pl.multiple_of(it * rc, rc)
        halves = [transpose_pieces([u_ref[time_rows(n0, hh * half + sp), :]
                                    for sp in range(half)]) for hh in range(2)]
        for g in range(gpt):
            uvec_scr[g, pl.ds(n0, rc), :] = jnp.concatenate(
                [halves[0][g], halves[1][g]], axis=1).astype(BF16)
        return carry

    lax.fori_loop(0, n_blocks // rc, gather, 0, unroll=S5_RELAYOUT_UNROLL)

    y_intra = [_dot(uvec_scr[g], toe_ref[g]) for g in range(gpt)]
    res, ims = [], []
    for k in range(gpt // 2):
        st = (_dot(uvec_scr[2 * k], wst_ref[2 * k])
              + _dot(uvec_scr[2 * k + 1], wst_ref[2 * k + 1]))
        res.append(st[:, :LANES])
        ims.append(st[:, LANES:])
    for r in range(int(math.log2(n_blocks))):
        dist = 1 << r
        for k in range(gpt // 2):
            lr = lam_r_ref[k, r:r + 1, :]
            li = lam_i_ref[k, r:r + 1, :]
            sre = jnp.where(row >= dist, pltpu.roll(res[k], dist, 0), 0.0)
            sim = jnp.where(row >= dist, pltpu.roll(ims[k], dist, 0), 0.0)
            res[k] = res[k] + sre * lr - sim * li
            ims[k] = ims[k] + sre * li + sim * lr
    for k in range(gpt // 2):
        prev = jnp.concatenate([jnp.where(row >= 1, pltpu.roll(res[k], 1, 0), 0.0),
                                jnp.where(row >= 1, pltpu.roll(ims[k], 1, 0), 0.0)], axis=1)
        prev = prev.astype(BF16)
        y_scr[2 * k] = y_intra[2 * k] + _dot(prev, wrd_ref[2 * k])
        y_scr[2 * k + 1] = y_intra[2 * k + 1] + _dot(prev, wrd_ref[2 * k + 1])

    dsk = d_ref[...]

    sub = lax.broadcasted_iota(jnp.int32, (SUBLANES, LANES), 0)

    def scatter(it, carry):
        n0 = pl.multiple_of(it * SUBLANES, SUBLANES)
        for hh in range(2):
            cols = slice(hh * LANES, (hh + 1) * LANES)
            by_time = transpose_pieces([y_scr[g, pl.ds(n0, SUBLANES), cols] for g in range(gpt)])
            by_block = butterfly_transpose(by_time, sub, 0, 1)
            for i in range(SUBLANES):
                rows = pl.ds(pl.multiple_of((n0 + i) * S5_BLOCK + hh * half, half), half)
                a_ref[rows, :] = _gelu_tanh(by_block[i] + dsk * u_ref[rows, :])
        return carry

    lax.fori_loop(0, n_blocks // SUBLANES, scatter, 0, unroll=S5_RELAYOUT_UNROLL)


def _s5_core(u, d_row, toe, w_state, w_read, lam_r, lam_i):
    s, d = u.shape
    n_blocks = s // S5_BLOCK
    gpt = S5_GROUPS_PER_TILE
    levels = lam_r.shape[1]
    return pl.pallas_call(
        functools.partial(_s5_core_kernel, n_blocks=n_blocks),
        grid=(d // LANES,),
        in_specs=[
            pl.BlockSpec((s, LANES), lambda i: (0, i)),
            pl.BlockSpec((1, LANES), lambda i: (0, i)),
            pl.BlockSpec((gpt, S5_BLOCK_W, S5_BLOCK_W), lambda i: (i, 0, 0)),
            pl.BlockSpec((gpt, S5_BLOCK_W, 4 * S5_STATE), lambda i: (i, 0, 0)),
            pl.BlockSpec((gpt, 4 * S5_STATE, S5_BLOCK_W), lambda i: (i, 0, 0)),
            pl.BlockSpec((gpt // 2, levels, 2 * S5_STATE), lambda i: (i, 0, 0)),
            pl.BlockSpec((gpt // 2, levels, 2 * S5_STATE), lambda i: (i, 0, 0)),
        ],
        out_specs=pl.BlockSpec((s, LANES), lambda i: (0, i)),
        out_shape=jax.ShapeDtypeStruct((s, d), F32),
        scratch_shapes=[
            pltpu.VMEM((gpt, n_blocks, S5_BLOCK_W), BF16),
            pltpu.VMEM((gpt, n_blocks, S5_BLOCK_W), F32),
        ],
        compiler_params=_cparams(1),
        name="s5_core",
    )(u, d_row, toe, w_state, w_read, lam_r, lam_i)


def _s5_operators(lam_re, lam_im, log_step, b_re, b_im, c_re, c_im, n_blocks):
    g = lam_re.shape[0]
    hi = lax.Precision.HIGHEST
    dt = jnp.exp(log_step)[:, None]
    ar, ai = lam_re * dt, lam_im * dt

    def lam_powers(exponents):
        mag = jnp.exp(ar[:, None, :] * exponents[None, :, None])
        ang = ai[:, None, :] * exponents[None, :, None]
        return mag * jnp.cos(ang), mag * jnp.sin(ang)

    pr, pi = lam_powers(jnp.arange(S5_BLOCK + 1, dtype=F32))
    nr, ni = pr[:, 1] - 1.0, pi[:, 1]
    den = lam_re * lam_re + lam_im * lam_im
    qr = (nr * lam_re + ni * lam_im) / den
    qi = (ni * lam_re - nr * lam_im) / den
    bbr = b_re * qr[..., None] - b_im * qi[..., None]
    bbi = b_re * qi[..., None] + b_im * qr[..., None]
    cpr = c_re[:, None] * pr[:, :, None, :] - c_im[:, None] * pi[:, :, None, :]
    cpi = c_re[:, None] * pi[:, :, None, :] + c_im[:, None] * pr[:, :, None, :]
    kern = (jnp.einsum('gtcp,gpd->gtcd', cpr[:, :S5_BLOCK], bbr, precision=hi)
            - jnp.einsum('gtcp,gpd->gtcd', cpi[:, :S5_BLOCK], bbi, precision=hi))
    s_idx = jnp.arange(S5_BLOCK)
    lag_is = (s_idx[None, None, :] - s_idx[None, :, None]
              == s_idx[:, None, None]).astype(F32)
    toe = jnp.einsum('ust,gucd->gsdtc', lag_is, kern, precision=hi)
    toe = toe.reshape(g, S5_BLOCK_W, S5_BLOCK_W).astype(BF16)
    fr = jnp.flip(pr[:, :S5_BLOCK], axis=1)[:, :, None, :]
    fi = jnp.flip(pi[:, :S5_BLOCK], axis=1)[:, :, None, :]
    bt_r = jnp.swapaxes(bbr, 1, 2)[:, None]
    bt_i = jnp.swapaxes(bbi, 1, 2)[:, None]
    bst_r = (fr * bt_r - fi * bt_i).reshape(g, S5_BLOCK_W, S5_STATE)
    bst_i = (fr * bt_i + fi * bt_r).reshape(g, S5_BLOCK_W, S5_STATE)
    even = (jnp.arange(g) % 2 == 0)[:, None, None]
    zb = jnp.zeros_like(bst_r)
    w_state = jnp.where(even, jnp.concatenate([bst_r, zb, bst_i, zb], axis=-1),
                        jnp.concatenate([zb, bst_r, zb, bst_i], axis=-1)).astype(BF16)
    rd_r = cpr[:, 1:].transpose(0, 3, 1, 2).reshape(g, S5_STATE, S5_BLOCK_W)
    rd_i = -cpi[:, 1:].transpose(0, 3, 1, 2).reshape(g, S5_STATE, S5_BLOCK_W)
    zc = jnp.zeros_like(rd_r)
    w_read = jnp.where(even, jnp.concatenate([rd_r, zc, rd_i, zc], axis=1),
                       jnp.concatenate([zc, rd_r, zc, rd_i], axis=1)).astype(BF16)
    levels = int(math.log2(n_blocks))
    sr, si = lam_powers(S5_BLOCK * (2.0 ** jnp.arange(levels, dtype=F32)))
    sr = sr.reshape(g // 2, 2, levels, S5_STATE)
    si = si.reshape(g // 2, 2, levels, S5_STATE)
    lam_r = jnp.concatenate([sr[:, 0], sr[:, 1]], axis=-1)
    lam_i = jnp.concatenate([si[:, 0], si[:, 1]], axis=-1)
    return toe, w_state, w_read, lam_r, lam_i


def _s5_out_kernel(h_ref, a_ref, wa_ref, wb_ref, o_ref, a_scr):
    @pl.when(pl.program_id(1) == 0)
    def _():
        a_scr[...] = a_ref[...].astype(BF16)

    a = a_scr[...]
    o_ref[...] = h_ref[...] + _dot(a, wa_ref[...]) * _sigmoid(_dot(a, wb_ref[...]))


def _s5_out(h, a, w_out, layer, *, tm, tn):
    s, d = h.shape
    nb = d // tn
    return pl.pallas_call(
        _s5_out_kernel,
        grid=(s // tm, nb),
        in_specs=[
            pl.BlockSpec((tm, tn), lambda i, j: (i, j)),
            pl.BlockSpec((tm, d), lambda i, j: (i, 0)),
            pl.BlockSpec((None, d, tn), lambda i, j: (layer, 0, j)),
            pl.BlockSpec((None, d, tn), lambda i, j: (layer, 0, nb + j)),
        ],
        out_specs=pl.BlockSpec((tm, tn), lambda i, j: (i, j)),
        out_shape=jax.ShapeDtypeStruct((s, d), F32),
        scratch_shapes=[pltpu.VMEM((tm, d), BF16)],
        compiler_params=_cparams(2),
        name="s5_out",
    )(h, a, w_out, w_out)


def _mlp_kernel(h_ref, g_ref, wu_ref, wd_ref, o_ref, hn_scr):
    @pl.when(pl.program_id(1) == 0)
    def _():
        h = h_ref[...]
        hn_scr[...] = _rms_rows(h, g_ref[...]).astype(BF16)
        o_ref[...] = h

    a = jnp.maximum(_dot(hn_scr[...], wu_ref[...]), 0.0)
    o_ref[...] += _dot((a * a).astype(BF16), wd_ref[...])


def _mlp(h, gain, w_up, w_down, layer, *, tm, tf):
    s, d = h.shape
    ff = w_up.shape[-1]
    return pl.pallas_call(
        _mlp_kernel,
        grid=(s // tm, ff // tf),
        in_specs=[
            pl.BlockSpec((tm, d), lambda i, k: (i, 0)),
            pl.BlockSpec((1, d), lambda i, k: (0, 0)),
            pl.BlockSpec((None, d, tf), lambda i, k: (layer, 0, k)),
            pl.BlockSpec((None, tf, d), lambda i, k: (layer, k, 0)),
        ],
        out_specs=pl.BlockSpec((tm, d), lambda i, k: (i, 0)),
        out_shape=jax.ShapeDtypeStruct((s, d), F32),
        scratch_shapes=[pltpu.VMEM((tm, d), BF16)],
        compiler_params=_cparams(2),
        name="mlp",
    )(h, gain, w_up, w_down)


def _ple_kernel(h_ref, g_ref, p_ref, wg_ref, wp_ref, o_ref, hn_scr, pb_scr, *, tn):
    j = pl.program_id(1)

    @pl.when(j == 0)
    def _():
        hn_scr[...] = _rms_rows(h_ref[...], g_ref[...]).astype(BF16)
        pb_scr[...] = p_ref[...].astype(BF16)

    gate = _sigmoid(_dot(hn_scr[...], wg_ref[...]))
    emb = _dot(pb_scr[...], wp_ref[...])
    cols = pl.ds(pl.multiple_of(j * tn, tn), tn)
    o_ref[...] = h_ref[:, cols] + gate * emb


def _ple(h, gain, p, layer, batch, w_gate, w_proj, *, tm, tn):
    s, d = h.shape
    pd = p.shape[-1]
    return pl.pallas_call(
        functools.partial(_ple_kernel, tn=tn),
        grid=(s // tm, d // tn),
        in_specs=[
            pl.BlockSpec((tm, d), lambda i, j: (i, 0)),
            pl.BlockSpec((1, d), lambda i, j: (0, 0)),
            pl.BlockSpec((None, None, tm, pd), lambda i, j: (layer, batch, i, 0)),
            pl.BlockSpec((None, d, tn), lambda i, j: (layer, 0, j)),
            pl.BlockSpec((None, pd, tn), lambda i, j: (layer, 0, j)),
        ],
        out_specs=pl.BlockSpec((tm, tn), lambda i, j: (i, j)),
        out_shape=jax.ShapeDtypeStruct((s, d), F32),
        scratch_shapes=[pltpu.VMEM((tm, d), BF16), pltpu.VMEM((tm, pd), BF16)],
        compiler_params=_cparams(2),
        name="ple",
    )(h, gain, p, w_gate, w_proj)


def _final_norm_kernel(h_ref, g_ref, o_ref):
    o_ref[...] = _rms_rows(h_ref[...], g_ref[...])


def _final_norm(h, gain, *, tm):
    s, d = h.shape
    return pl.pallas_call(
        _final_norm_kernel,
        grid=(s // tm,),
        in_specs=[pl.BlockSpec((tm, d), lambda i: (i, 0)),
                  pl.BlockSpec((1, d), lambda i: (0, 0))],
        out_specs=pl.BlockSpec((tm, d), lambda i: (i, 0)),
        out_shape=jax.ShapeDtypeStruct((s, d), F32),
        compiler_params=_cparams(1),
        name="final_norm",
    )(h, gain)


def _tile(n, want):
    t = min(n, want)
    while n % t:
        t //= 2
    return t


def _gdn_layer(h, gain, w_in, w_in_f32, layer, conv_w, a_log, dt_bias, o_norm, w_out):
    s, d = h.shape
    tm = _tile(s, 1024)
    pad_v = GATE_LANES - 2 * GDN_V_HEADS
    w_ba = jnp.pad(w_in_f32[layer, :, GDN_CONV_DIM + GDN_VAL_DIM:].astype(BF16),
                   ((0, 0), (0, pad_v)))
    alog_row = jnp.concatenate([jnp.zeros((GDN_V_HEADS,), F32), a_log,
                                jnp.zeros((pad_v,), F32)])[None, :]
    dtb_row = jnp.concatenate([jnp.zeros((GDN_V_HEADS,), F32), dt_bias,
                               jnp.zeros((pad_v,), F32)])[None, :]
    qkv = _gdn_qkv(h, gain, w_in, layer, conv_w, tm=tm, tn=512)
    zs, gates_c = _gdn_zgate(h, gain, w_in, layer, w_ba, alog_row, dtb_row, tm=tm, tn=512)
    gates_r = gates_c.T
    on = _gdn_core(qkv, zs, gates_c, gates_r, o_norm[None, :], tb=_tile(s, 512))
    return _res_matmul(h, on, w_out, layer, tm=tm, tn=512)


def _s5_layer(h, gain, w_in, layer, lam_re, lam_im, log_step, b_re, b_im, c_re, c_im, d_skip,
              w_out):
    s, d = h.shape
    u = _norm_matmul(h, gain, w_in, layer, tm=_tile(s, 1024), tn=512)
    operators = _s5_operators(lam_re, lam_im, log_step, b_re, b_im, c_re, c_im, s // S5_BLOCK)
    a = _s5_core(u, d_skip[None, :], *operators)
    return _s5_out(h, a, w_out, layer, tm=_tile(s, 1024), tn=512)


def kernel(x, p, norm_mix, norm_mlp, norm_ple, norm_final, gdn_w_in, gdn_conv_w, gdn_a_log,
           gdn_dt_bias, gdn_o_norm, gdn_w_out, s5_w_in, s5_lam_re, s5_lam_im, s5_log_step,
           s5_b_re, s5_b_im, s5_c_re, s5_c_im, s5_d, s5_w_out, mlp_w_up, mlp_w_down,
           ple_w_proj, ple_w_gate):
    bsz, s, d = x.shape
    depth = norm_mix.shape[0]
    gdn_w_in_b, gdn_w_out_b = gdn_w_in.astype(BF16), gdn_w_out.astype(BF16)
    s5_w_in_b, s5_w_out_b = s5_w_in.astype(BF16), s5_w_out.astype(BF16)
    mlp_w_up_b, mlp_w_down_b = mlp_w_up.astype(BF16), mlp_w_down.astype(BF16)
    ple_w_gate_b, ple_w_proj_b = ple_w_gate.astype(BF16), ple_w_proj.astype(BF16)
    outs = []
    for b in range(bsz):
        h = x[b]
        for i in range(depth):
            j = i // 2
            if i % 2 == 0:
                h = _gdn_layer(h, norm_mix[i][None, :], gdn_w_in_b, gdn_w_in, j, gdn_conv_w[j],
                               gdn_a_log[j], gdn_dt_bias[j], gdn_o_norm[j], gdn_w_out_b)
            else:
                h = _s5_layer(h, norm_mix[i][None, :], s5_w_in_b, j, s5_lam_re[j], s5_lam_im[j],
                              s5_log_step[j], s5_b_re[j], s5_b_im[j], s5_c_re[j], s5_c_im[j],
                              s5_d[j], s5_w_out_b)
            h = _mlp(h, norm_mlp[i][None, :], mlp_w_up_b, mlp_w_down_b, i,
                     tm=_tile(s, 1024), tf=512)
            h = _ple(h, norm_ple[i][None, :], p, i, b, ple_w_gate_b, ple_w_proj_b,
                     tm=_tile(s, 1024), tn=512)
        outs.append(_final_norm(h, norm_final[None, :], tm=_tile(s, 512)))
    return jnp.stack(outs, axis=0)
```

```python
import functools
import math

import jax
import jax.numpy as jnp
from jax import lax
from jax.experimental import pallas as pl
from jax.experimental.pallas import tpu as pltpu

F32 = jnp.float32
BF16 = jnp.bfloat16

NORM_EPS = 1e-6
L2_EPS = 1e-6

GDN_K_HEADS = 16
GDN_V_HEADS = 32
GDN_HEAD = 128
GDN_CONV = 4
GDN_KEY_DIM = GDN_K_HEADS * GDN_HEAD
GDN_VAL_DIM = GDN_V_HEADS * GDN_HEAD
GDN_CONV_DIM = 2 * GDN_KEY_DIM + GDN_VAL_DIM
GDN_CHUNK = 64
GDN_GROUP = 4 * GDN_CHUNK
EPILOGUE_ROW_SUBBLOCK = 256
GDN_STAGE_SKEW = 2
QKV_ROW_SUBBLOCK = 128
GATE_LANES = 128

S5_GROUP_CH = 16
S5_STATE = 64
S5_BLOCK = 16
S5_BLOCK_W = S5_BLOCK * S5_GROUP_CH

SUBLANES = 8
LANES = 128
S5_GROUPS_PER_TILE = LANES // S5_GROUP_CH
S5_ROW_CHUNK = 16
S5_RELAYOUT_UNROLL = 4
VMEM_LIMIT_BYTES = 56 * 1024 * 1024


def _cparams(n_axes):
    return pltpu.CompilerParams(
        dimension_semantics=("arbitrary",) * n_axes,
        vmem_limit_bytes=VMEM_LIMIT_BYTES)


def _rms_rows(x, gain):
    ms = jnp.mean(x * x, axis=-1, keepdims=True)
    return x * lax.rsqrt(ms + NORM_EPS) * gain


def _sigmoid(x):
    return 1.0 / (1.0 + jnp.exp(-x))


def _softplus(x):
    return jnp.maximum(x, 0.0) + jnp.log(1.0 + jnp.exp(-jnp.abs(x)))


def _gelu_tanh(x):
    c = math.sqrt(2.0 / math.pi)
    return 0.5 * x * (1.0 + jnp.tanh(c * (x + 0.044715 * (x * x * x))))


def _dot(a, b):
    return jnp.dot(a, b, preferred_element_type=F32)


def _gdn_qkv_kernel(h_ref, g_ref, w_ref, cw_ref, o_ref, hn_scr, tail_scr, buf_scr,
                    *, tm, tn, rb, n_qk_blocks):
    i = pl.program_id(0)
    j = pl.program_id(1)

    @pl.when(j == 0)
    def _():
        hn_scr[...] = _rms_rows(h_ref[...], g_ref[...]).astype(BF16)

    @pl.when(i == 0)
    def _():
        tail_scr[j] = jnp.zeros((SUBLANES, tn), F32)

    buf_scr[0:SUBLANES, :] = tail_scr[j]
    cw = cw_ref[...]

    def row_blocks(l2_norm):
        if l2_norm:
            scale = jnp.where(j < n_qk_blocks // 2, GDN_HEAD ** -0.5, 1.0).astype(F32)
        for r in range(tm // rb):
            rows = slice(r * rb, (r + 1) * rb)
            acc = _dot(hn_scr[rows, :], w_ref[...])
            buf_scr[SUBLANES + r * rb:SUBLANES + (r + 1) * rb, :] = acc
            y = acc * cw[3:4, :]
            for tap in range(GDN_CONV - 1):
                shift = GDN_CONV - 1 - tap
                y = y + buf_scr[pl.ds(SUBLANES + r * rb - shift, rb), :] * cw[tap:tap + 1, :]
            y = y * _sigmoid(y)
            if not l2_norm:
                o_ref[rows, :] = y
                continue
            for hh in range(tn // GDN_HEAD):
                sl = slice(hh * GDN_HEAD, (hh + 1) * GDN_HEAD)
                xs = y[:, sl]
                inv = lax.rsqrt(jnp.sum(xs * xs, axis=-1, keepdims=True) + L2_EPS)
                o_ref[rows, sl] = xs * (inv * scale)
        tail_scr[j] = buf_scr[tm:tm + SUBLANES, :]

    pl.when(j < n_qk_blocks)(functools.partial(row_blocks, True))
    pl.when(j >= n_qk_blocks)(functools.partial(row_blocks, False))


def _gdn_qkv(h, gain, w_in, layer, conv_w, *, tm, tn):
    s, d = h.shape
    n = conv_w.shape[1]
    kern = functools.partial(_gdn_qkv_kernel, tm=tm, tn=tn, rb=_tile(tm, QKV_ROW_SUBBLOCK),
                             n_qk_blocks=2 * GDN_KEY_DIM // tn)
    return pl.pallas_call(
        kern,
        grid=(s // tm, n // tn),
        in_specs=[
            pl.BlockSpec((tm, d), lambda i, j: (i, 0)),
            pl.BlockSpec((1, d), lambda i, j: (0, 0)),
            pl.BlockSpec((None, d, tn), lambda i, j: (layer, 0, j)),
            pl.BlockSpec((GDN_CONV, tn), lambda i, j: (0, j)),
        ],
        out_specs=pl.BlockSpec((tm, tn), lambda i, j: (i, j)),
        out_shape=jax.ShapeDtypeStruct((s, n), F32),
        scratch_shapes=[
            pltpu.VMEM((tm, d), BF16),
            pltpu.VMEM((n // tn, SUBLANES, tn), F32),
            pltpu.VMEM((tm + SUBLANES, tn), F32),
        ],
        compiler_params=_cparams(2),
        name="gdn_qkv",
    )(h, gain, w_in, conv_w)


def _gdn_zgate_kernel(h_ref, g_ref, wz_ref, wba_ref, alog_ref, dtb_ref, tri_ref,
                      z_ref, gate_ref, hn_scr, *, tm):
    j = pl.program_id(1)

    @pl.when(j == 0)
    def _():
        hn = _rms_rows(h_ref[...], g_ref[...]).astype(BF16)
        hn_scr[...] = hn
        ba = _dot(hn, wba_ref[...])
        beta = _sigmoid(ba)
        g = -jnp.exp(alog_ref[...]) * _softplus(ba + dtb_ref[...])
        lane = lax.broadcasted_iota(jnp.int32, (GDN_CHUNK, GATE_LANES), 1)
        tri = tri_ref[...]
        for c in range(tm // GDN_CHUNK):
            rows = slice(c * GDN_CHUNK, (c + 1) * GDN_CHUNK)
            gc = jnp.dot(tri, g[rows, :], preferred_element_type=F32,
                         precision=lax.Precision.HIGHEST)
            gate_ref[rows, :] = jnp.where(lane < GDN_V_HEADS, beta[rows, :], gc)

    rb = _tile(tm, QKV_ROW_SUBBLOCK)
    for r in range(tm // rb):
        rows = slice(r * rb, (r + 1) * rb)
        z = _dot(hn_scr[rows, :], wz_ref[...])
        z_ref[rows, :] = z * _sigmoid(z)


def _gdn_zgate(h, gain, w_in, layer, w_ba, alog_row, dtb_row, *, tm, tn):
    s, d = h.shape
    n = GDN_VAL_DIM
    z_off = GDN_CONV_DIM // tn
    tri = jnp.tril(jnp.ones((GDN_CHUNK, GDN_CHUNK), F32))
    return pl.pallas_call(
        functools.partial(_gdn_zgate_kernel, tm=tm),
        grid=(s // tm, n // tn),
        in_specs=[
            pl.BlockSpec((tm, d), lambda i, j: (i, 0)),
            pl.BlockSpec((1, d), lambda i, j: (0, 0)),
            pl.BlockSpec((None, d, tn), lambda i, j: (layer, 0, z_off + j)),
            pl.BlockSpec((d, GATE_LANES), lambda i, j: (0, 0)),
            pl.BlockSpec((1, GATE_LANES), lambda i, j: (0, 0)),
            pl.BlockSpec((1, GATE_LANES), lambda i, j: (0, 0)),
            pl.BlockSpec((GDN_CHUNK, GDN_CHUNK), lambda i, j: (0, 0)),
        ],
        out_specs=[
            pl.BlockSpec((tm, tn), lambda i, j: (i, j)),
            pl.BlockSpec((tm, GATE_LANES), lambda i, j: (i, 0)),
        ],
        out_shape=[
            jax.ShapeDtypeStruct((s, n), F32),
            jax.ShapeDtypeStruct((s, GATE_LANES), F32),
        ],
        scratch_shapes=[pltpu.VMEM((tm, d), BF16)],
        compiler_params=_cparams(2),
        name="gdn_zgate",
    )(h, gain, w_in, w_ba, alog_row, dtb_row, tri)


def _gdn_core_kernel(q_ref, k_ref, v_ref, z_ref, gc_ref, gr_ref, on_ref, o_ref,
                     s_scr, mq_scr, nn_scr, oi_scr, dl_scr, *, tb, n_t):
    kh = pl.program_id(0)
    t = pl.program_id(1)
    c_sz = GDN_CHUNK
    grp = GDN_GROUP
    n_in_grp = grp // c_sz
    n_chunks = tb // c_sz
    onorm = on_ref[...]

    @pl.when(t == 0)
    def _():
        s_scr[...] = jnp.zeros_like(s_scr)
        mq_scr[...] = jnp.zeros_like(mq_scr)
        nn_scr[...] = jnp.zeros_like(nn_scr)
        oi_scr[...] = jnp.zeros_like(oi_scr)
        dl_scr[...] = jnp.zeros_like(dl_scr)

    def recurrence_step(state, c):
        rows = slice(c * c_sz, (c + 1) * c_sz)
        for e in range(2):
            idx = e * n_chunks + c
            vcols = slice(e * GDN_HEAD, (e + 1) * GDN_HEAD)
            r = _dot(mq_scr[idx], state[e].astype(BF16))
            o = r[GDN_HEAD:] + oi_scr[idx]
            state[e] = state[e] * dl_scr[idx][0:1, :] - r[:GDN_HEAD] + nn_scr[idx]
            ms = jnp.mean(o * o, axis=-1, keepdims=True)
            o_ref[rows, vcols] = (o * lax.rsqrt(ms + NORM_EPS) * onorm
                                  * z_ref[rows, vcols]).astype(BF16)

    def drain_only():
        state = [s_scr[0], s_scr[1]]
        for c in range(n_chunks):
            recurrence_step(state, c)

    pl.when(t == n_t)(drain_only)
    pl.when(t < n_t)(functools.partial(
        _gdn_core_block, q_ref, k_ref, v_ref, gc_ref, gr_ref, s_scr, mq_scr, nn_scr, oi_scr,
        dl_scr, recurrence_step, kh, tb))


def _gdn_core_block(q_ref, k_ref, v_ref, gc_ref, gr_ref, s_scr, mq_scr, nn_scr, oi_scr,
                    dl_scr, recurrence_step, kh, tb):
    c_sz = GDN_CHUNK
    grp = GDN_GROUP
    n_in_grp = grp // c_sz
    n_chunks = tb // c_sz
    state = [s_scr[0], s_scr[1]]
    pending = list(range(n_chunks))

    def advance(count=1):
        for _ in range(min(count, len(pending))):
            recurrence_step(state, pending.pop(0))

    gates = gc_ref[...]
    lane = lax.broadcasted_iota(jnp.int32, (tb, GATE_LANES), 1)
    ri = lax.broadcasted_iota(jnp.int32, (grp, grp), 0)
    ci = lax.broadcasted_iota(jnp.int32, (grp, grp), 1)
    same_chunk = (ri // c_sz) == (ci // c_sz)
    bd_causal = same_chunk & (ri >= ci)
    bd_strict = same_chunk & (ri > ci)
    bd_mask = jnp.where(same_chunk, 1.0, 0.0).astype(BF16)
    pi = lax.broadcasted_iota(jnp.int32, (c_sz, grp), 0)
    pj = lax.broadcasted_iota(jnp.int32, (c_sz, grp), 1)
    pc = pj % c_sz
    eye_p = jnp.where(pi == pc, 1.0, 0.0).astype(F32)
    merge_sizes = [1 << lv for lv in range(int(math.log2(c_sz)))]
    merge_mask = {b: ((pi // (2 * b)) == (pc // (2 * b))) & ((pi // b) != (pc // b))
                  for b in merge_sizes}
    nt_dims = (((1,), (1,)), ((), ()))
    tn_dims = (((0,), (0,)), ((), ()))

    def block_diag(packed):
        return jnp.concatenate([packed] * n_in_grp, axis=0) * bd_mask

    per_head = []
    for e in range(2):
        head = 2 * kh + e
        beta_all = jnp.sum(jnp.where(lane == head, gates, 0.0), axis=-1, keepdims=True)
        gc_all = jnp.sum(jnp.where(lane == GDN_V_HEADS + head, gates, 0.0),
                         axis=-1, keepdims=True)
        gr_all = gr_ref[pl.ds(GDN_V_HEADS + head, 1), :]
        per_head.append((beta_all, gc_all, gr_all))

    new_ops = []

    def group_stages(g):
        rows = slice(g * grp, (g + 1) * grp)
        st = {}

        def gram():
            kg = k_ref[rows, :]
            qg = q_ref[rows, :]
            kb = kg.astype(BF16)
            st.update(kg=kg, qg=qg,
                      gkk=lax.dot_general(kb, kb, nt_dims, preferred_element_type=F32),
                      gqk=lax.dot_general(qg.astype(BF16), kb, nt_dims,
                                          preferred_element_type=F32))

        def setup():
            for e in range(2):
                beta_all, gc_all, gr_all = per_head[e]
                bcol = beta_all[rows, :]
                gcol = gc_all[rows, :]
                decay = jnp.exp(jnp.minimum(gcol - gr_all[:, rows], 0.0))
                a_nat = jnp.where(bd_strict, bcol * st["gkk"] * decay, 0.0)
                a_p = (a_nat[0:c_sz] + a_nat[c_sz:2 * c_sz]
                       + a_nat[2 * c_sz:3 * c_sz] + a_nat[3 * c_sz:4 * c_sz])
                st[e] = dict(bcol=bcol, gcol=gcol, decay=decay, a_p=a_p,
                             t_inv=eye_p - jnp.where(merge_mask[1], a_p, 0.0))

        def merge_left(b):
            for e in range(2):
                hd = st[e]
                join = jnp.where(merge_mask[b], hd["a_p"], 0.0).astype(BF16)
                hd["tl"] = _dot(hd["t_inv"].astype(BF16), block_diag(join)).astype(BF16)

        def merge_right(b):
            for e in range(2):
                hd = st[e]
                hd["t_inv"] = hd["t_inv"] - _dot(hd["tl"], block_diag(hd["t_inv"].astype(BF16)))

        def solve():
            for e in range(2):
                hd = st[e]
                vcols = slice(e * GDN_HEAD, (e + 1) * GDN_HEAD)
                eg = jnp.exp(hd["gcol"])
                rhs = jnp.concatenate([st["kg"] * (hd["bcol"] * eg),
                                       v_ref[rows, vcols] * hd["bcol"]], axis=1).astype(BF16)
                hd["eg"] = eg
                hd["wu"] = _dot(block_diag(hd["t_inv"].astype(BF16)), rhs).astype(BF16)

        def intra():
            for e in range(2):
                hd = st[e]
                qkm = jnp.where(bd_causal, st["gqk"] * hd["decay"], 0.0).astype(BF16)
                hd["qw"] = _dot(qkm, hd["wu"])

        def chunk_operators():
            for m in range(n_in_grp):
                cr = slice(m * c_sz, (m + 1) * c_sz)
                for e in range(2):
                    hd = st[e]
                    gcol = hd["gcol"]
                    glast = gcol[(m + 1) * c_sz - 1:(m + 1) * c_sz, :]
                    kd = (st["kg"][cr] * jnp.exp(glast - gcol[cr])).astype(BF16)
                    mn = lax.dot_general(kd, hd["wu"][cr], tn_dims,
                                         preferred_element_type=F32)
                    q_eff = (st["qg"][cr] * hd["eg"][cr]
                             - hd["qw"][cr, :GDN_HEAD]).astype(BF16)
                    new_ops.append((
                        e * n_chunks + g * n_in_grp + m,
                        jnp.concatenate([mn[:, :GDN_HEAD].astype(BF16), q_eff], axis=0),
                        mn[:, GDN_HEAD:],
                        hd["qw"][cr, GDN_HEAD:],
                        jnp.broadcast_to(jnp.exp(glast), (SUBLANES, GDN_HEAD))))

        merges = []
        for b in merge_sizes[1:]:
            merges += [functools.partial(merge_left, b), functools.partial(merge_right, b)]
        return [gram, setup] + merges + [solve, intra, chunk_operators]

    stage_lists = [group_stages(g) for g in range(tb // grp)]
    n_stages = len(stage_lists[0])
    n_slots = n_stages + GDN_STAGE_SKEW * (len(stage_lists) - 1)
    for slot in range(n_slots):
        for g, stages in enumerate(stage_lists):
            k = slot - GDN_STAGE_SKEW * g
            if 0 <= k < n_stages:
                stages[k]()
        advance(-(-n_chunks // n_slots))
    advance(n_chunks)
    s_scr[0] = state[0]
    s_scr[1] = state[1]
    for idx, mq_v, nn_v, oi_v, dl_v in new_ops:
        mq_scr[idx] = mq_v
        nn_scr[idx] = nn_v
        oi_scr[idx] = oi_v
        dl_scr[idx] = dl_v


def _gdn_core(qkv, zs, gates_c, gates_r, o_norm_row, *, tb):
    s = qkv.shape[0]
    nk = GDN_K_HEADS
    n_t = s // tb
    n_ops = 2 * (tb // GDN_CHUNK)

    def cur(kh, t):
        return jnp.minimum(t, n_t - 1)

    def prev(kh, t):
        return jnp.maximum(t - 1, 0)

    return pl.pallas_call(
        functools.partial(_gdn_core_kernel, tb=tb, n_t=n_t),
        grid=(nk, n_t + 1),
        in_specs=[
            pl.BlockSpec((tb, GDN_HEAD), lambda kh, t: (cur(kh, t), kh)),
            pl.BlockSpec((tb, GDN_HEAD), lambda kh, t: (cur(kh, t), nk + kh)),
            pl.BlockSpec((tb, 2 * GDN_HEAD), lambda kh, t: (cur(kh, t), nk + kh)),
            pl.BlockSpec((tb, 2 * GDN_HEAD), lambda kh, t: (prev(kh, t), kh)),
            pl.BlockSpec((tb, GATE_LANES), lambda kh, t: (cur(kh, t), 0)),
            pl.BlockSpec((GATE_LANES, tb), lambda kh, t: (0, cur(kh, t))),
            pl.BlockSpec((1, GDN_HEAD), lambda kh, t: (0, 0)),
        ],
        out_specs=pl.BlockSpec((tb, 2 * GDN_HEAD), lambda kh, t: (prev(kh, t), kh)),
        out_shape=jax.ShapeDtypeStruct((s, GDN_VAL_DIM), BF16),
        scratch_shapes=[
            pltpu.VMEM((2, GDN_HEAD, GDN_HEAD), F32),
            pltpu.VMEM((n_ops, GDN_HEAD + GDN_CHUNK, GDN_HEAD), BF16),
            pltpu.VMEM((n_ops, GDN_HEAD, GDN_HEAD), F32),
            pltpu.VMEM((n_ops, GDN_CHUNK, GDN_HEAD), F32),
            pltpu.VMEM((n_ops, SUBLANES, GDN_HEAD), F32),
        ],
        compiler_params=_cparams(2),
        name="gdn_core",
    )(qkv, qkv, qkv, zs, gates_c, gates_r, o_norm_row)


def _res_matmul_kernel(h_ref, a_ref, w_ref, o_ref):
    o_ref[...] = h_ref[...] + _dot(a_ref[...], w_ref[...])


def _res_matmul(h, a, w, layer, *, tm, tn):
    s, d = h.shape
    kdim = a.shape[1]
    return pl.pallas_call(
        _res_matmul_kernel,
        grid=(s // tm, d // tn),
        in_specs=[
            pl.BlockSpec((tm, tn), lambda i, j: (i, j)),
            pl.BlockSpec((tm, kdim), lambda i, j: (i, 0)),
            pl.BlockSpec((None, kdim, tn), lambda i, j: (layer, 0, j)),
        ],
        out_specs=pl.BlockSpec((tm, tn), lambda i, j: (i, j)),
        out_shape=jax.ShapeDtypeStruct((s, d), F32),
        compiler_params=_cparams(2),
        name="res_matmul",
    )(h, a, w)


def _norm_matmul_kernel(h_ref, g_ref, w_ref, o_ref, hn_scr):
    @pl.when(pl.program_id(1) == 0)
    def _():
        hn_scr[...] = _rms_rows(h_ref[...], g_ref[...]).astype(BF16)

    o_ref[...] = _dot(hn_scr[...], w_ref[...])


def _norm_matmul(h, gain, w, layer, *, tm, tn):
    s, d = h.shape
    n = w.shape[-1]
    return pl.pallas_call(
        _norm_matmul_kernel,
        grid=(s // tm, n // tn),
        in_specs=[
            pl.BlockSpec((tm, d), lambda i, j: (i, 0)),
            pl.BlockSpec((1, d), lambda i, j: (0, 0)),
            pl.BlockSpec((None, d, tn), lambda i, j: (layer, 0, j)),
        ],
        out_specs=pl.BlockSpec((tm, tn), lambda i, j: (i, j)),
        out_shape=jax.ShapeDtypeStruct((s, n), F32),
        scratch_shapes=[pltpu.VMEM((tm, d), BF16)],
        compiler_params=_cparams(2),
        name="s5_in_proj",
    )(h, gain, w)


def _s5_core_kernel(u_ref, d_ref, toe_ref, wst_ref, wrd_ref, lam_r_ref, lam_i_ref, a_ref,
                    uvec_scr, y_scr, *, n_blocks):
    gpt = S5_GROUPS_PER_TILE
    half = S5_BLOCK // 2
    rc = S5_ROW_CHUNK
    piece = lax.broadcasted_iota(jnp.int32, (rc, LANES), 1) // S5_GROUP_CH
    row = lax.broadcasted_iota(jnp.int32, (n_blocks, LANES), 0)

    def time_rows(n0, step):
        return pl.ds(n0 * S5_BLOCK + step, rc, stride=S5_BLOCK)

    def butterfly_transpose(vs, index, axis, unit):
        vs = list(vs)
        size = 8 * unit
        for dist in (4, 2, 1):
            low = (index // dist) % 2 == 0
            nxt = list(vs)
            for i in range(len(vs)):
                if (i // dist) % 2 == 0:
                    j = i + dist
                    nxt[i] = jnp.where(low, vs[i], pltpu.roll(vs[j], unit * dist, axis))
                    nxt[j] = jnp.where(low, pltpu.roll(vs[i], size - unit * dist, axis), vs[j])
            vs = nxt
        return vs

    def transpose_pieces(vs):
        return butterfly_transpose(vs, piece[:vs[0].shape[0]], 1, S5_GROUP_CH)

    def gather(it, carry):
        n0 = pl.multiple_of(it * rc, rc)
        halves = [transpose_pieces([u_ref[time_rows(n0, hh * half + sp), :]
                                    for sp in range(half)]) for hh in range(2)]
        for g in range(gpt):
            uvec_scr[g, pl.ds(n0, rc), :] = jnp.concatenate(
                [halves[0][g], halves[1][g]], axis=1).astype(BF16)
        return carry

    lax.fori_loop(0, n_blocks // rc, gather, 0, unroll=S5_RELAYOUT_UNROLL)

    y_intra = [_dot(uvec_scr[g], toe_ref[g]) for g in range(gpt)]
    res, ims = [], []
    for k in range(gpt // 2):
        st = (_dot(uvec_scr[2 * k], wst_ref[2 * k])
              + _dot(uvec_scr[2 * k + 1], wst_ref[2 * k + 1]))
        res.append(st[:, :LANES])
        ims.append(st[:, LANES:])
    for r in range(int(math.log2(n_blocks))):
        dist = 1 << r
        for k in range(gpt // 2):
            lr = lam_r_ref[k, r:r + 1, :]
            li = lam_i_ref[k, r:r + 1, :]
            sre = jnp.where(row >= dist, pltpu.roll(res[k], dist, 0), 0.0)
            sim = jnp.where(row >= dist, pltpu.roll(ims[k], dist, 0), 0.0)
            res[k] = res[k] + sre * lr - sim * li
            ims[k] = ims[k] + sre * li + sim * lr
    for k in range(gpt // 2):
        prev = jnp.concatenate([jnp.where(row >= 1, pltpu.roll(res[k], 1, 0), 0.0),
                                jnp.where(row >= 1, pltpu.roll(ims[k], 1, 0), 0.0)], axis=1)
        prev = prev.astype(BF16)
        y_scr[2 * k] = y_intra[2 * k] + _dot(prev, wrd_ref[2 * k])
        y_scr[2 * k + 1] = y_intra[2 * k + 1] + _dot(prev, wrd_ref[2 * k + 1])

    dsk = d_ref[...]

    sub = lax.broadcasted_iota(jnp.int32, (SUBLANES, LANES), 0)

    def scatter(it, carry):
        n0 = pl.multiple_of(it * SUBLANES, SUBLANES)
        for hh in range(2):
            cols = slice(hh * LANES, (hh + 1) * LANES)
            by_time = transpose_pieces([y_scr[g, pl.ds(n0, SUBLANES), cols] for g in range(gpt)])
            by_block = butterfly_transpose(by_time, sub, 0, 1)
            for i in range(SUBLANES):
                rows = pl.ds(pl.multiple_of((n0 + i) * S5_BLOCK + hh * half, half), half)
                a_ref[rows, :] = _gelu_tanh(by_block[i] + dsk * u_ref[rows, :])
        return carry

    lax.fori_loop(0, n_blocks // SUBLANES, scatter, 0, unroll=S5_RELAYOUT_UNROLL)


def _s5_core(u, d_row, toe, w_state, w_read, lam_r, lam_i):
    s, d = u.shape
    n_blocks = s // S5_BLOCK
    gpt = S5_GROUPS_PER_TILE
    levels = lam_r.shape[1]
    return pl.pallas_call(
        functools.partial(_s5_core_kernel, n_blocks=n_blocks),
        grid=(d // LANES,),
        in_specs=[
            pl.BlockSpec((s, LANES), lambda i: (0, i)),
            pl.BlockSpec((1, LANES), lambda i: (0, i)),
            pl.BlockSpec((gpt, S5_BLOCK_W, S5_BLOCK_W), lambda i: (i, 0, 0)),
            pl.BlockSpec((gpt, S5_BLOCK_W, 4 * S5_STATE), lambda i: (i, 0, 0)),
            pl.BlockSpec((gpt, 4 * S5_STATE, S5_BLOCK_W), lambda i: (i, 0, 0)),
            pl.BlockSpec((gpt // 2, levels, 2 * S5_STATE), lambda i: (i, 0, 0)),
            pl.BlockSpec((gpt // 2, levels, 2 * S5_STATE), lambda i: (i, 0, 0)),
        ],
        out_specs=pl.BlockSpec((s, LANES), lambda i: (0, i)),
        out_shape=jax.ShapeDtypeStruct((s, d), F32),
        scratch_shapes=[
            pltpu.VMEM((gpt, n_blocks, S5_BLOCK_W), BF16),
            pltpu.VMEM((gpt, n_blocks, S5_BLOCK_W), F32),
        ],
        compiler_params=_cparams(1),
        name="s5_core",
    )(u, d_row, toe, w_state, w_read, lam_r, lam_i)


def _s5_operators(lam_re, lam_im, log_step, b_re, b_im, c_re, c_im, n_blocks):
    g = lam_re.shape[0]
    hi = lax.Precision.HIGHEST
    dt = jnp.exp(log_step)[:, None]
    ar, ai = lam_re * dt, lam_im * dt

    def lam_powers(exponents):
        mag = jnp.exp(ar[:, None, :] * exponents[None, :, None])
        ang = ai[:, None, :] * exponents[None, :, None]
        return mag * jnp.cos(ang), mag * jnp.sin(ang)

    pr, pi = lam_powers(jnp.arange(S5_BLOCK + 1, dtype=F32))
    nr, ni = pr[:, 1] - 1.0, pi[:, 1]
    den = lam_re * lam_re + lam_im * lam_im
    qr = (nr * lam_re + ni * lam_im) / den
    qi = (ni * lam_re - nr * lam_im) / den
    bbr = b_re * qr[..., None] - b_im * qi[..., None]
    bbi = b_re * qi[..., None] + b_im * qr[..., None]
    cpr = c_re[:, None] * pr[:, :, None, :] - c_im[:, None] * pi[:, :, None, :]
    cpi = c_re[:, None] * pi[:, :, None, :] + c_im[:, None] * pr[:, :, None, :]
    kern = (jnp.einsum('gtcp,gpd->gtcd', cpr[:, :S5_BLOCK], bbr, precision=hi)
            - jnp.einsum('gtcp,gpd->gtcd', cpi[:, :S5_BLOCK], bbi, precision=hi))
    s_idx = jnp.arange(S5_BLOCK)
    lag_is = (s_idx[None, None, :] - s_idx[None, :, None]
              == s_idx[:, None, None]).astype(F32)
    toe = jnp.einsum('ust,gucd->gsdtc', lag_is, kern, precision=hi)
    toe = toe.reshape(g, S5_BLOCK_W, S5_BLOCK_W).astype(BF16)
    fr = jnp.flip(pr[:, :S5_BLOCK], axis=1)[:, :, None, :]
    fi = jnp.flip(pi[:, :S5_BLOCK], axis=1)[:, :, None, :]
    bt_r = jnp.swapaxes(bbr, 1, 2)[:, None]
    bt_i = jnp.swapaxes(bbi, 1, 2)[:, None]
    bst_r = (fr * bt_r - fi * bt_i).reshape(g, S5_BLOCK_W, S5_STATE)
    bst_i = (fr * bt_i + fi * bt_r).reshape(g, S5_BLOCK_W, S5_STATE)
    even = (jnp.arange(g) % 2 == 0)[:, None, None]
    zb = jnp.zeros_like(bst_r)
    w_state = jnp.where(even, jnp.concatenate([bst_r, zb, bst_i, zb], axis=-1),
                        jnp.concatenate([zb, bst_r, zb, bst_i], axis=-1)).astype(BF16)
    rd_r = cpr[:, 1:].transpose(0, 3, 1, 2).reshape(g, S5_STATE, S5_BLOCK_W)
    rd_i = -cpi[:, 1:].transpose(0, 3, 1, 2).reshape(g, S5_STATE, S5_BLOCK_W)
    zc = jnp.zeros_like(rd_r)
    w_read = jnp.where(even, jnp.concatenate([rd_r, zc, rd_i, zc], axis=1),
                       jnp.concatenate([zc, rd_r, zc, rd_i], axis=1)).astype(BF16)
    levels = int(math.log2(n_blocks))
    sr, si = lam_powers(S5_BLOCK * (2.0 ** jnp.arange(levels, dtype=F32)))
    sr = sr.reshape(g // 2, 2, levels, S5_STATE)
    si = si.reshape(g // 2, 2, levels, S5_STATE)
    lam_r = jnp.concatenate([sr[:, 0], sr[:, 1]], axis=-1)
    lam_i = jnp.concatenate([si[:, 0], si[:, 1]], axis=-1)
    return toe, w_state, w_read, lam_r, lam_i


def _s5_out_kernel(h_ref, a_ref, wa_ref, wb_ref, o_ref, a_scr):
    @pl.when(pl.program_id(1) == 0)
    def _():
        a_scr[...] = a_ref[...].astype(BF16)

    tm = a_scr.shape[0]
    rb = _tile(tm, EPILOGUE_ROW_SUBBLOCK)
    for r in range(tm // rb):
        rows = slice(r * rb, (r + 1) * rb)
        a = a_scr[rows, :]
        o_ref[rows, :] = (h_ref[rows, :]
                          + _dot(a, wa_ref[...]) * _sigmoid(_dot(a, wb_ref[...])))


def _s5_out(h, a, w_out, layer, *, tm, tn):
    s, d = h.shape
    nb = d // tn
    return pl.pallas_call(
        _s5_out_kernel,
        grid=(s // tm, nb),
        in_specs=[
            pl.BlockSpec((tm, tn), lambda i, j: (i, j)),
            pl.BlockSpec((tm, d), lambda i, j: (i, 0)),
            pl.BlockSpec((None, d, tn), lambda i, j: (layer, 0, j)),
            pl.BlockSpec((None, d, tn), lambda i, j: (layer, 0, nb + j)),
        ],
        out_specs=pl.BlockSpec((tm, tn), lambda i, j: (i, j)),
        out_shape=jax.ShapeDtypeStruct((s, d), F32),
        scratch_shapes=[pltpu.VMEM((tm, d), BF16)],
        compiler_params=_cparams(2),
        name="s5_out",
    )(h, a, w_out, w_out)


def _mlp_kernel(h_ref, g_ref, wu_ref, wd_ref, o_ref, hn_scr):
    @pl.when(pl.program_id(1) == 0)
    def _():
        h = h_ref[...]
        hn_scr[...] = _rms_rows(h, g_ref[...]).astype(BF16)
        o_ref[...] = h

    a = jnp.maximum(_dot(hn_scr[...], wu_ref[...]), 0.0)
    o_ref[...] += _dot((a * a).astype(BF16), wd_ref[...])


def _mlp(h, gain, w_up, w_down, layer, *, tm, tf):
    s, d = h.shape
    ff = w_up.shape[-1]
    return pl.pallas_call(
        _mlp_kernel,
        grid=(s // tm, ff // tf),
        in_specs=[
            pl.BlockSpec((tm, d), lambda i, k: (i, 0)),
            pl.BlockSpec((1, d), lambda i, k: (0, 0)),
            pl.BlockSpec((None, d, tf), lambda i, k: (layer, 0, k)),
            pl.BlockSpec((None, tf, d), lambda i, k: (layer, k, 0)),
        ],
        out_specs=pl.BlockSpec((tm, d), lambda i, k: (i, 0)),
        out_shape=jax.ShapeDtypeStruct((s, d), F32),
        scratch_shapes=[pltpu.VMEM((tm, d), BF16)],
        compiler_params=_cparams(2),
        name="mlp",
    )(h, gain, w_up, w_down)


def _ple_kernel(h_ref, g_ref, p_ref, wg_ref, wp_ref, o_ref, hn_scr, pb_scr, *, tn):
    j = pl.program_id(1)

    @pl.when(j == 0)
    def _():
        hn_scr[...] = _rms_rows(h_ref[...], g_ref[...]).astype(BF16)
        pb_scr[...] = p_ref[...].astype(BF16)

    cols = pl.ds(pl.multiple_of(j * tn, tn), tn)
    tm = hn_scr.shape[0]
    rb = _tile(tm, EPILOGUE_ROW_SUBBLOCK)
    for r in range(tm // rb):
        rows = slice(r * rb, (r + 1) * rb)
        gate = _sigmoid(_dot(hn_scr[rows, :], wg_ref[...]))
        emb = _dot(pb_scr[rows, :], wp_ref[...])
        o_ref[rows, :] = h_ref[rows, cols] + gate * emb


def _ple(h, gain, p, layer, batch, w_gate, w_proj, *, tm, tn):
    s, d = h.shape
    pd = p.shape[-1]
    return pl.pallas_call(
        functools.partial(_ple_kernel, tn=tn),
        grid=(s // tm, d // tn),
        in_specs=[
            pl.BlockSpec((tm, d), lambda i, j: (i, 0)),
            pl.BlockSpec((1, d), lambda i, j: (0, 0)),
            pl.BlockSpec((None, None, tm, pd), lambda i, j: (layer, batch, i, 0)),
            pl.BlockSpec((None, d, tn), lambda i, j: (layer, 0, j)),
            pl.BlockSpec((None, pd, tn), lambda i, j: (layer, 0, j)),
        ],
        out_specs=pl.BlockSpec((tm, tn), lambda i, j: (i, j)),
        out_shape=jax.ShapeDtypeStruct((s, d), F32),
        scratch_shapes=[pltpu.VMEM((tm, d), BF16), pltpu.VMEM((tm, pd), BF16)],
        compiler_params=_cparams(2),
        name="ple",
    )(h, gain, p, w_gate, w_proj)


def _final_norm_kernel(h_ref, g_ref, o_ref):
    o_ref[...] = _rms_rows(h_ref[...], g_ref[...])


def _final_norm(h, gain, *, tm):
    s, d = h.shape
    return pl.pallas_call(
        _final_norm_kernel,
        grid=(s // tm,),
        in_specs=[pl.BlockSpec((tm, d), lambda i: (i, 0)),
                  pl.BlockSpec((1, d), lambda i: (0, 0))],
        out_specs=pl.BlockSpec((tm, d), lambda i: (i, 0)),
        out_shape=jax.ShapeDtypeStruct((s, d), F32),
        compiler_params=_cparams(1),
        name="final_norm",
    )(h, gain)


def _tile(n, want):
    t = min(n, want)
    while n % t:
        t //= 2
    return t


def _gdn_layer(h, gain, w_in, w_in_f32, layer, conv_w, a_log, dt_bias, o_norm, w_out):
    s, d = h.shape
    tm = _tile(s, 1024)
    pad_v = GATE_LANES - 2 * GDN_V_HEADS
    w_ba = jnp.pad(w_in_f32[layer, :, GDN_CONV_DIM + GDN_VAL_DIM:].astype(BF16),
                   ((0, 0), (0, pad_v)))
    alog_row = jnp.concatenate([jnp.zeros((GDN_V_HEADS,), F32), a_log,
                                jnp.zeros((pad_v,), F32)])[None, :]
    dtb_row = jnp.concatenate([jnp.zeros((GDN_V_HEADS,), F32), dt_bias,
                               jnp.zeros((pad_v,), F32)])[None, :]
    qkv = _gdn_qkv(h, gain, w_in, layer, conv_w, tm=tm, tn=512)
    zs, gates_c = _gdn_zgate(h, gain, w_in, layer, w_ba, alog_row, dtb_row, tm=tm, tn=512)
    gates_r = gates_c.T
    on = _gdn_core(qkv, zs, gates_c, gates_r, o_norm[None, :], tb=_tile(s, 1024))
    return _res_matmul(h, on, w_out, layer, tm=tm, tn=512)


def _s5_layer(h, gain, w_in, layer, lam_re, lam_im, log_step, b_re, b_im, c_re, c_im, d_skip,
              w_out):
    s, d = h.shape
    u = _norm_matmul(h, gain, w_in, layer, tm=_tile(s, 1024), tn=512)
    operators = _s5_operators(lam_re, lam_im, log_step, b_re, b_im, c_re, c_im, s // S5_BLOCK)
    a = _s5_core(u, d_skip[None, :], *operators)
    return _s5_out(h, a, w_out, layer, tm=_tile(s, 1024), tn=512)


def kernel(x, p, norm_mix, norm_mlp, norm_ple, norm_final, gdn_w_in, gdn_conv_w, gdn_a_log,
           gdn_dt_bias, gdn_o_norm, gdn_w_out, s5_w_in, s5_lam_re, s5_lam_im, s5_log_step,
           s5_b_re, s5_b_im, s5_c_re, s5_c_im, s5_d, s5_w_out, mlp_w_up, mlp_w_down,
           ple_w_proj, ple_w_gate):
    bsz, s, d = x.shape
    depth = norm_mix.shape[0]
    gdn_w_in_b, gdn_w_out_b = gdn_w_in.astype(BF16), gdn_w_out.astype(BF16)
    s5_w_in_b, s5_w_out_b = s5_w_in.astype(BF16), s5_w_out.astype(BF16)
    mlp_w_up_b, mlp_w_down_b = mlp_w_up.astype(BF16), mlp_w_down.astype(BF16)
    ple_w_gate_b, ple_w_proj_b = ple_w_gate.astype(BF16), ple_w_proj.astype(BF16)
    outs = []
    for b in range(bsz):
        h = x[b]
        for i in range(depth):
            j = i // 2
            if i % 2 == 0:
                h = _gdn_layer(h, norm_mix[i][None, :], gdn_w_in_b, gdn_w_in, j, gdn_conv_w[j],
                               gdn_a_log[j], gdn_dt_bias[j], gdn_o_norm[j], gdn_w_out_b)
            else:
                h = _s5_layer(h, norm_mix[i][None, :], s5_w_in_b, j, s5_lam_re[j], s5_lam_im[j],
                              s5_log_step[j], s5_b_re[j], s5_b_im[j], s5_c_re[j], s5_c_im[j],
                              s5_d[j], s5_w_out_b)
            h = _mlp(h, norm_mlp[i][None, :], mlp_w_up_b, mlp_w_down_b, i,
                     tm=_tile(s, 1024), tf=512)
            h = _ple(h, norm_ple[i][None, :], p, i, b, ple_w_gate_b, ple_w_proj_b,
                     tm=_tile(s, 1024), tn=512)
        outs.append(_final_norm(h, norm_final[None, :], tm=_tile(s, 512)))
    return jnp.stack(outs, axis=0)
```

```python
import functools
import math

import jax
import jax.numpy as jnp
from jax import lax
from jax.experimental import pallas as pl
from jax.experimental.pallas import tpu as pltpu

F32 = jnp.float32
BF16 = jnp.bfloat16

NORM_EPS = 1e-6
L2_EPS = 1e-6

GDN_K_HEADS = 16
GDN_V_HEADS = 32
GDN_HEAD = 128
GDN_CONV = 4
GDN_KEY_DIM = GDN_K_HEADS * GDN_HEAD
GDN_VAL_DIM = GDN_V_HEADS * GDN_HEAD
GDN_CONV_DIM = 2 * GDN_KEY_DIM + GDN_VAL_DIM
GDN_CHUNK = 64
GDN_GROUP = 4 * GDN_CHUNK
EPILOGUE_ROW_SUBBLOCK = 256
GDN_STAGE_SKEW = 2
QKV_ROW_SUBBLOCK = 256
ZGATE_ROW_SUBBLOCK = 128
GATE_LANES = 128

S5_GROUP_CH = 16
S5_STATE = 64
S5_BLOCK = 16
S5_BLOCK_W = S5_BLOCK * S5_GROUP_CH

SUBLANES = 8
LANES = 128
S5_GROUPS_PER_TILE = LANES // S5_GROUP_CH
S5_ROW_CHUNK = 16
S5_RELAYOUT_UNROLL = 4
VMEM_LIMIT_BYTES = 56 * 1024 * 1024


def _cparams(n_axes):
    return pltpu.CompilerParams(
        dimension_semantics=("arbitrary",) * n_axes,
        vmem_limit_bytes=VMEM_LIMIT_BYTES)


def _rms_rows(x, gain):
    ms = jnp.mean(x * x, axis=-1, keepdims=True)
    return x * lax.rsqrt(ms + NORM_EPS) * gain


def _sigmoid(x):
    return 1.0 / (1.0 + jnp.exp(-x))


def _softplus(x):
    return jnp.maximum(x, 0.0) + jnp.log(1.0 + jnp.exp(-jnp.abs(x)))


def _gelu_tanh(x):
    c = math.sqrt(2.0 / math.pi)
    return 0.5 * x * (1.0 + jnp.tanh(c * (x + 0.044715 * (x * x * x))))


def _dot(a, b):
    return jnp.dot(a, b, preferred_element_type=F32)


def _gdn_qkv_kernel(h_ref, g_ref, w_ref, cw_ref, o_ref, hn_scr, tail_scr, buf_scr,
                    *, tm, tn, rb, n_qk_blocks):
    i = pl.program_id(0)
    j = pl.program_id(1)

    @pl.when(j == 0)
    def _():
        hn_scr[...] = _rms_rows(h_ref[...], g_ref[...]).astype(BF16)

    @pl.when(i == 0)
    def _():
        tail_scr[j] = jnp.zeros((SUBLANES, tn), F32)

    buf_scr[0:SUBLANES, :] = tail_scr[j]
    cw = cw_ref[...]

    def row_blocks(l2_norm):
        if l2_norm:
            scale = jnp.where(j < n_qk_blocks // 2, GDN_HEAD ** -0.5, 1.0).astype(F32)
        for r in range(tm // rb):
            rows = slice(r * rb, (r + 1) * rb)
            acc = _dot(hn_scr[rows, :], w_ref[...])
            buf_scr[SUBLANES + r * rb:SUBLANES + (r + 1) * rb, :] = acc
            y = acc * cw[3:4, :]
            for tap in range(GDN_CONV - 1):
                shift = GDN_CONV - 1 - tap
                y = y + buf_scr[pl.ds(SUBLANES + r * rb - shift, rb), :] * cw[tap:tap + 1, :]
            y = y * _sigmoid(y)
            if not l2_norm:
                o_ref[rows, :] = y
                continue
            for hh in range(tn // GDN_HEAD):
                sl = slice(hh * GDN_HEAD, (hh + 1) * GDN_HEAD)
                xs = y[:, sl]
                inv = lax.rsqrt(jnp.sum(xs * xs, axis=-1, keepdims=True) + L2_EPS)
                o_ref[rows, sl] = xs * (inv * scale)
        tail_scr[j] = buf_scr[tm:tm + SUBLANES, :]

    pl.when(j < n_qk_blocks)(functools.partial(row_blocks, True))
    pl.when(j >= n_qk_blocks)(functools.partial(row_blocks, False))


def _gdn_qkv(h, gain, w_in, layer, conv_w, *, tm, tn):
    s, d = h.shape
    n = conv_w.shape[1]
    kern = functools.partial(_gdn_qkv_kernel, tm=tm, tn=tn, rb=_tile(tm, QKV_ROW_SUBBLOCK),
                             n_qk_blocks=2 * GDN_KEY_DIM // tn)
    return pl.pallas_call(
        kern,
        grid=(s // tm, n // tn),
        in_specs=[
            pl.BlockSpec((tm, d), lambda i, j: (i, 0)),
            pl.BlockSpec((1, d), lambda i, j: (0, 0)),
            pl.BlockSpec((None, d, tn), lambda i, j: (layer, 0, j)),
            pl.BlockSpec((GDN_CONV, tn), lambda i, j: (0, j)),
        ],
        out_specs=pl.BlockSpec((tm, tn), lambda i, j: (i, j)),
        out_shape=jax.ShapeDtypeStruct((s, n), F32),
        scratch_shapes=[
            pltpu.VMEM((tm, d), BF16),
            pltpu.VMEM((n // tn, SUBLANES, tn), F32),
            pltpu.VMEM((tm + SUBLANES, tn), F32),
        ],
        compiler_params=_cparams(2),
        name="gdn_qkv",
    )(h, gain, w_in, conv_w)


def _gdn_zgate_kernel(h_ref, g_ref, wz_ref, wba_ref, alog_ref, dtb_ref, tri_ref,
                      z_ref, gate_ref, hn_scr, *, tm):
    j = pl.program_id(1)

    @pl.when(j == 0)
    def _():
        hn = _rms_rows(h_ref[...], g_ref[...]).astype(BF16)
        hn_scr[...] = hn
        ba = _dot(hn, wba_ref[...])
        beta = _sigmoid(ba)
        g = -jnp.exp(alog_ref[...]) * _softplus(ba + dtb_ref[...])
        lane = lax.broadcasted_iota(jnp.int32, (GDN_CHUNK, GATE_LANES), 1)
        tri = tri_ref[...]
        for c in range(tm // GDN_CHUNK):
            rows = slice(c * GDN_CHUNK, (c + 1) * GDN_CHUNK)
            gc = jnp.dot(tri, g[rows, :], preferred_element_type=F32,
                         precision=lax.Precision.HIGHEST)
            gate_ref[rows, :] = jnp.where(lane < GDN_V_HEADS, beta[rows, :], gc)

    rb = _tile(tm, ZGATE_ROW_SUBBLOCK)
    for r in range(tm // rb):
        rows = slice(r * rb, (r + 1) * rb)
        z = _dot(hn_scr[rows, :], wz_ref[...])
        z_ref[rows, :] = z * _sigmoid(z)


def _gdn_zgate(h, gain, w_in, layer, w_ba, alog_row, dtb_row, *, tm, tn):
    s, d = h.shape
    n = GDN_VAL_DIM
    z_off = GDN_CONV_DIM // tn
    tri = jnp.tril(jnp.ones((GDN_CHUNK, GDN_CHUNK), F32))
    return pl.pallas_call(
        functools.partial(_gdn_zgate_kernel, tm=tm),
        grid=(s // tm, n // tn),
        in_specs=[
            pl.BlockSpec((tm, d), lambda i, j: (i, 0)),
            pl.BlockSpec((1, d), lambda i, j: (0, 0)),
            pl.BlockSpec((None, d, tn), lambda i, j: (layer, 0, z_off + j)),
            pl.BlockSpec((d, GATE_LANES), lambda i, j: (0, 0)),
            pl.BlockSpec((1, GATE_LANES), lambda i, j: (0, 0)),
            pl.BlockSpec((1, GATE_LANES), lambda i, j: (0, 0)),
            pl.BlockSpec((GDN_CHUNK, GDN_CHUNK), lambda i, j: (0, 0)),
        ],
        out_specs=[
            pl.BlockSpec((tm, tn), lambda i, j: (i, j)),
            pl.BlockSpec((tm, GATE_LANES), lambda i, j: (i, 0)),
        ],
        out_shape=[
            jax.ShapeDtypeStruct((s, n), F32),
            jax.ShapeDtypeStruct((s, GATE_LANES), F32),
        ],
        scratch_shapes=[pltpu.VMEM((tm, d), BF16)],
        compiler_params=_cparams(2),
        name="gdn_zgate",
    )(h, gain, w_in, w_ba, alog_row, dtb_row, tri)


def _gdn_core_kernel(q_ref, k_ref, v_ref, z_ref, gc_ref, gr_ref, on_ref, o_ref,
                     s_scr, mq_scr, nn_scr, oi_scr, dl_scr, *, tb, n_t):
    kh = pl.program_id(0)
    t = pl.program_id(1)
    c_sz = GDN_CHUNK
    grp = GDN_GROUP
    n_in_grp = grp // c_sz
    n_chunks = tb // c_sz
    onorm = on_ref[...]

    @pl.when(t == 0)
    def _():
        s_scr[...] = jnp.zeros_like(s_scr)
        mq_scr[...] = jnp.zeros_like(mq_scr)
        nn_scr[...] = jnp.zeros_like(nn_scr)
        oi_scr[...] = jnp.zeros_like(oi_scr)
        dl_scr[...] = jnp.zeros_like(dl_scr)

    def recurrence_step(state, c):
        rows = slice(c * c_sz, (c + 1) * c_sz)
        for e in range(2):
            idx = e * n_chunks + c
            vcols = slice(e * GDN_HEAD, (e + 1) * GDN_HEAD)
            r = _dot(mq_scr[idx], state[e].astype(BF16))
            o = r[GDN_HEAD:] + oi_scr[idx]
            state[e] = state[e] * dl_scr[idx][0:1, :] - r[:GDN_HEAD] + nn_scr[idx]
            ms = jnp.mean(o * o, axis=-1, keepdims=True)
            o_ref[rows, vcols] = (o * lax.rsqrt(ms + NORM_EPS) * onorm
                                  * z_ref[rows, vcols]).astype(BF16)

    def drain_only():
        state = [s_scr[0], s_scr[1]]
        for c in range(n_chunks):
            recurrence_step(state, c)

    pl.when(t == n_t)(drain_only)
    pl.when(t < n_t)(functools.partial(
        _gdn_core_block, q_ref, k_ref, v_ref, gc_ref, gr_ref, s_scr, mq_scr, nn_scr, oi_scr,
        dl_scr, recurrence_step, kh, tb))


def _gdn_core_block(q_ref, k_ref, v_ref, gc_ref, gr_ref, s_scr, mq_scr, nn_scr, oi_scr,
                    dl_scr, recurrence_step, kh, tb):
    c_sz = GDN_CHUNK
    grp = GDN_GROUP
    n_in_grp = grp // c_sz
    n_chunks = tb // c_sz
    state = [s_scr[0], s_scr[1]]
    pending = list(range(n_chunks))

    def advance(count=1):
        for _ in range(min(count, len(pending))):
            recurrence_step(state, pending.pop(0))

    gates = gc_ref[...]
    lane = lax.broadcasted_iota(jnp.int32, (tb, GATE_LANES), 1)
    ri = lax.broadcasted_iota(jnp.int32, (grp, grp), 0)
    ci = lax.broadcasted_iota(jnp.int32, (grp, grp), 1)
    same_chunk = (ri // c_sz) == (ci // c_sz)
    bd_causal = same_chunk & (ri >= ci)
    bd_strict = same_chunk & (ri > ci)
    bd_mask = jnp.where(same_chunk, 1.0, 0.0).astype(BF16)
    pi = lax.broadcasted_iota(jnp.int32, (c_sz, grp), 0)
    pj = lax.broadcasted_iota(jnp.int32, (c_sz, grp), 1)
    pc = pj % c_sz
    eye_p = jnp.where(pi == pc, 1.0, 0.0).astype(F32)
    merge_sizes = [1 << lv for lv in range(int(math.log2(c_sz)))]
    merge_mask = {b: ((pi // (2 * b)) == (pc // (2 * b))) & ((pi // b) != (pc // b))
                  for b in merge_sizes}
    nt_dims = (((1,), (1,)), ((), ()))
    tn_dims = (((0,), (0,)), ((), ()))

    def block_diag(packed):
        return jnp.concatenate([packed] * n_in_grp, axis=0) * bd_mask

    per_head = []
    for e in range(2):
        head = 2 * kh + e
        beta_all = jnp.sum(jnp.where(lane == head, gates, 0.0), axis=-1, keepdims=True)
        gc_all = jnp.sum(jnp.where(lane == GDN_V_HEADS + head, gates, 0.0),
                         axis=-1, keepdims=True)
        gr_all = gr_ref[pl.ds(GDN_V_HEADS + head, 1), :]
        per_head.append((beta_all, gc_all, gr_all))

    new_ops = []

    def group_stages(g):
        rows = slice(g * grp, (g + 1) * grp)
        st = {}

        def gram():
            kg = k_ref[rows, :]
            qg = q_ref[rows, :]
            kb = kg.astype(BF16)
            st.update(kg=kg, qg=qg,
                      gkk=lax.dot_general(kb, kb, nt_dims, preferred_element_type=F32),
                      gqk=lax.dot_general(qg.astype(BF16), kb, nt_dims,
                                          preferred_element_type=F32))

        def setup():
            for e in range(2):
                beta_all, gc_all, gr_all = per_head[e]
                bcol = beta_all[rows, :]
                gcol = gc_all[rows, :]
                decay = jnp.exp(jnp.minimum(gcol - gr_all[:, rows], 0.0))
                a_nat = jnp.where(bd_strict, bcol * st["gkk"] * decay, 0.0)
                a_p = (a_nat[0:c_sz] + a_nat[c_sz:2 * c_sz]
                       + a_nat[2 * c_sz:3 * c_sz] + a_nat[3 * c_sz:4 * c_sz])
                st[e] = dict(bcol=bcol, gcol=gcol, decay=decay, a_p=a_p,
                             t_inv=eye_p - jnp.where(merge_mask[1], a_p, 0.0))

        def merge_left(b):
            for e in range(2):
                hd = st[e]
                join = jnp.where(merge_mask[b], hd["a_p"], 0.0).astype(BF16)
                hd["tl"] = _dot(hd["t_inv"].astype(BF16), block_diag(join)).astype(BF16)

        def merge_right(b):
            for e in range(2):
                hd = st[e]
                hd["t_inv"] = hd["t_inv"] - _dot(hd["tl"], block_diag(hd["t_inv"].astype(BF16)))

        def solve():
            for e in range(2):
                hd = st[e]
                vcols = slice(e * GDN_HEAD, (e + 1) * GDN_HEAD)
                eg = jnp.exp(hd["gcol"])
                rhs = jnp.concatenate([st["kg"] * (hd["bcol"] * eg),
                                       v_ref[rows, vcols] * hd["bcol"]], axis=1).astype(BF16)
                hd["eg"] = eg
                hd["wu"] = _dot(block_diag(hd["t_inv"].astype(BF16)), rhs).astype(BF16)

        def intra():
            for e in range(2):
                hd = st[e]
                qkm = jnp.where(bd_causal, st["gqk"] * hd["decay"], 0.0).astype(BF16)
                hd["qw"] = _dot(qkm, hd["wu"])

        def chunk_operators():
            for m in range(n_in_grp):
                cr = slice(m * c_sz, (m + 1) * c_sz)
                for e in range(2):
                    hd = st[e]
                    gcol = hd["gcol"]
                    glast = gcol[(m + 1) * c_sz - 1:(m + 1) * c_sz, :]
                    kd = (st["kg"][cr] * jnp.exp(glast - gcol[cr])).astype(BF16)
                    mn = lax.dot_general(kd, hd["wu"][cr], tn_dims,
                                         preferred_element_type=F32)
                    q_eff = (st["qg"][cr] * hd["eg"][cr]
                             - hd["qw"][cr, :GDN_HEAD]).astype(BF16)
                    new_ops.append((
                        e * n_chunks + g * n_in_grp + m,
                        jnp.concatenate([mn[:, :GDN_HEAD].astype(BF16), q_eff], axis=0),
                        mn[:, GDN_HEAD:],
                        hd["qw"][cr, GDN_HEAD:],
                        jnp.broadcast_to(jnp.exp(glast), (SUBLANES, GDN_HEAD))))

        merges = []
        for b in merge_sizes[1:]:
            merges += [functools.partial(merge_left, b), functools.partial(merge_right, b)]
        return [gram, setup] + merges + [solve, intra, chunk_operators]

    stage_lists = [group_stages(g) for g in range(tb // grp)]
    n_stages = len(stage_lists[0])
    n_slots = n_stages + GDN_STAGE_SKEW * (len(stage_lists) - 1)
    for slot in range(n_slots):
        for g, stages in enumerate(stage_lists):
            k = slot - GDN_STAGE_SKEW * g
            if 0 <= k < n_stages:
                stages[k]()
        advance(-(-n_chunks // n_slots))
    advance(n_chunks)
    s_scr[0] = state[0]
    s_scr[1] = state[1]
    for idx, mq_v, nn_v, oi_v, dl_v in new_ops:
        mq_scr[idx] = mq_v
        nn_scr[idx] = nn_v
        oi_scr[idx] = oi_v
        dl_scr[idx] = dl_v


def _gdn_core(qkv, zs, gates_c, gates_r, o_norm_row, *, tb):
    s = qkv.shape[0]
    nk = GDN_K_HEADS
    n_t = s // tb
    n_ops = 2 * (tb // GDN_CHUNK)

    def cur(kh, t):
        return jnp.minimum(t, n_t - 1)

    def prev(kh, t):
        return jnp.maximum(t - 1, 0)

    return pl.pallas_call(
        functools.partial(_gdn_core_kernel, tb=tb, n_t=n_t),
        grid=(nk, n_t + 1),
        in_specs=[
            pl.BlockSpec((tb, GDN_HEAD), lambda kh, t: (cur(kh, t), kh)),
            pl.BlockSpec((tb, GDN_HEAD), lambda kh, t: (cur(kh, t), nk + kh)),
            pl.BlockSpec((tb, 2 * GDN_HEAD), lambda kh, t: (cur(kh, t), nk + kh)),
            pl.BlockSpec((tb, 2 * GDN_HEAD), lambda kh, t: (prev(kh, t), kh)),
            pl.BlockSpec((tb, GATE_LANES), lambda kh, t: (cur(kh, t), 0)),
            pl.BlockSpec((GATE_LANES, tb), lambda kh, t: (0, cur(kh, t))),
            pl.BlockSpec((1, GDN_HEAD), lambda kh, t: (0, 0)),
        ],
        out_specs=pl.BlockSpec((tb, 2 * GDN_HEAD), lambda kh, t: (prev(kh, t), kh)),
        out_shape=jax.ShapeDtypeStruct((s, GDN_VAL_DIM), BF16),
        scratch_shapes=[
            pltpu.VMEM((2, GDN_HEAD, GDN_HEAD), F32),
            pltpu.VMEM((n_ops, GDN_HEAD + GDN_CHUNK, GDN_HEAD), BF16),
            pltpu.VMEM((n_ops, GDN_HEAD, GDN_HEAD), F32),
            pltpu.VMEM((n_ops, GDN_CHUNK, GDN_HEAD), F32),
            pltpu.VMEM((n_ops, SUBLANES, GDN_HEAD), F32),
        ],
        compiler_params=_cparams(2),
        name="gdn_core",
    )(qkv, qkv, qkv, zs, gates_c, gates_r, o_norm_row)


def _res_matmul_kernel(h_ref, a_ref, w_ref, o_ref):
    o_ref[...] = h_ref[...] + _dot(a_ref[...], w_ref[...])


def _res_matmul(h, a, w, layer, *, tm, tn):
    s, d = h.shape
    kdim = a.shape[1]
    return pl.pallas_call(
        _res_matmul_kernel,
        grid=(s // tm, d // tn),
        in_specs=[
            pl.BlockSpec((tm, tn), lambda i, j: (i, j)),
            pl.BlockSpec((tm, kdim), lambda i, j: (i, 0)),
            pl.BlockSpec((None, kdim, tn), lambda i, j: (layer, 0, j)),
        ],
        out_specs=pl.BlockSpec((tm, tn), lambda i, j: (i, j)),
        out_shape=jax.ShapeDtypeStruct((s, d), F32),
        compiler_params=_cparams(2),
        name="res_matmul",
    )(h, a, w)


def _norm_matmul_kernel(h_ref, g_ref, w_ref, o_ref, hn_scr):
    @pl.when(pl.program_id(1) == 0)
    def _():
        hn_scr[...] = _rms_rows(h_ref[...], g_ref[...]).astype(BF16)

    o_ref[...] = _dot(hn_scr[...], w_ref[...])


def _norm_matmul(h, gain, w, layer, *, tm, tn):
    s, d = h.shape
    n = w.shape[-1]
    return pl.pallas_call(
        _norm_matmul_kernel,
        grid=(s // tm, n // tn),
        in_specs=[
            pl.BlockSpec((tm, d), lambda i, j: (i, 0)),
            pl.BlockSpec((1, d), lambda i, j: (0, 0)),
            pl.BlockSpec((None, d, tn), lambda i, j: (layer, 0, j)),
        ],
        out_specs=pl.BlockSpec((tm, tn), lambda i, j: (i, j)),
        out_shape=jax.ShapeDtypeStruct((s, n), F32),
        scratch_shapes=[pltpu.VMEM((tm, d), BF16)],
        compiler_params=_cparams(2),
        name="s5_in_proj",
    )(h, gain, w)


def _s5_core_kernel(u_ref, d_ref, toe_ref, wst_ref, wrd_ref, lam_r_ref, lam_i_ref, a_ref,
                    uvec_scr, y_scr, *, n_blocks):
    gpt = S5_GROUPS_PER_TILE
    half = S5_BLOCK // 2
    rc = S5_ROW_CHUNK
    piece = lax.broadcasted_iota(jnp.int32, (rc, LANES), 1) // S5_GROUP_CH
    row = lax.broadcasted_iota(jnp.int32, (n_blocks, LANES), 0)

    def time_rows(n0, step):
        return pl.ds(n0 * S5_BLOCK + step, rc, stride=S5_BLOCK)

    def butterfly_transpose(vs, index, axis, unit):
        vs = list(vs)
        size = 8 * unit
        for dist in (4, 2, 1):
            low = (index // dist) % 2 == 0
            nxt = list(vs)
            for i in range(len(vs)):
                if (i // dist) % 2 == 0:
                    j = i + dist
                    nxt[i] = jnp.where(low, vs[i], pltpu.roll(vs[j], unit * dist, axis))
                    nxt[j] = jnp.where(low, pltpu.roll(vs[i], size - unit * dist, axis), vs[j])
            vs = nxt
        return vs

    def transpose_pieces(vs):
        return butterfly_transpose(vs, piece[:vs[0].shape[0]], 1, S5_GROUP_CH)

    def gather(it, carry):
        n0 = pl.multiple_of(it * rc, rc)
        halves = [transpose_pieces([u_ref[time_rows(n0, hh * half + sp), :]
                                    for sp in range(half)]) for hh in range(2)]
        for g in range(gpt):
            uvec_scr[g, pl.ds(n0, rc), :] = jnp.concatenate(
                [halves[0][g], halves[1][g]], axis=1).astype(BF16)
        return carry

    lax.fori_loop(0, n_blocks // rc, gather, 0, unroll=S5_RELAYOUT_UNROLL)

    y_intra = [_dot(uvec_scr[g], toe_ref[g]) for g in range(gpt)]
    res, ims = [], []
    for k in range(gpt // 2):
        st = (_dot(uvec_scr[2 * k], wst_ref[2 * k])
              + _dot(uvec_scr[2 * k + 1], wst_ref[2 * k + 1]))
        res.append(st[:, :LANES])
        ims.append(st[:, LANES:])
    for r in range(int(math.log2(n_blocks))):
        dist = 1 << r
        for k in range(gpt // 2):
            lr = lam_r_ref[k, r:r + 1, :]
            li = lam_i_ref[k, r:r + 1, :]
            sre = jnp.where(row >= dist, pltpu.roll(res[k], dist, 0), 0.0)
            sim = jnp.where(row >= dist, pltpu.roll(ims[k], dist, 0), 0.0)
            res[k] = res[k] + sre * lr - sim * li
            ims[k] = ims[k] + sre * li + sim * lr
    for k in range(gpt // 2):
        prev = jnp.concatenate([jnp.where(row >= 1, pltpu.roll(res[k], 1, 0), 0.0),
                                jnp.where(row >= 1, pltpu.roll(ims[k], 1, 0), 0.0)], axis=1)
        prev = prev.astype(BF16)
        y_scr[2 * k] = y_intra[2 * k] + _dot(prev, wrd_ref[2 * k])
        y_scr[2 * k + 1] = y_intra[2 * k + 1] + _dot(prev, wrd_ref[2 * k + 1])

    dsk = d_ref[...]

    sub = lax.broadcasted_iota(jnp.int32, (SUBLANES, LANES), 0)

    def scatter(it, carry):
        n0 = pl.multiple_of(it * SUBLANES, SUBLANES)
        for hh in range(2):
            cols = slice(hh * LANES, (hh + 1) * LANES)
            by_time = transpose_pieces([y_scr[g, pl.ds(n0, SUBLANES), cols] for g in range(gpt)])
            by_block = butterfly_transpose(by_time, sub, 0, 1)
            for i in range(SUBLANES):
                rows = pl.ds(pl.multiple_of((n0 + i) * S5_BLOCK + hh * half, half), half)
                a_ref[rows, :] = _gelu_tanh(by_block[i] + dsk * u_ref[rows, :])
        return carry

    lax.fori_loop(0, n_blocks // SUBLANES, scatter, 0, unroll=S5_RELAYOUT_UNROLL)


def _s5_core(u, d_row, toe, w_state, w_read, lam_r, lam_i):
    s, d = u.shape
    n_blocks = s // S5_BLOCK
    gpt = S5_GROUPS_PER_TILE
    levels = lam_r.shape[1]
    return pl.pallas_call(
        functools.partial(_s5_core_kernel, n_blocks=n_blocks),
        grid=(d // LANES,),
        in_specs=[
            pl.BlockSpec((s, LANES), lambda i: (0, i)),
            pl.BlockSpec((1, LANES), lambda i: (0, i)),
            pl.BlockSpec((gpt, S5_BLOCK_W, S5_BLOCK_W), lambda i: (i, 0, 0)),
            pl.BlockSpec((gpt, S5_BLOCK_W, 4 * S5_STATE), lambda i: (i, 0, 0)),
            pl.BlockSpec((gpt, 4 * S5_STATE, S5_BLOCK_W), lambda i: (i, 0, 0)),
            pl.BlockSpec((gpt // 2, levels, 2 * S5_STATE), lambda i: (i, 0, 0)),
            pl.BlockSpec((gpt // 2, levels, 2 * S5_STATE), lambda i: (i, 0, 0)),
        ],
        out_specs=pl.BlockSpec((s, LANES), lambda i: (0, i)),
        out_shape=jax.ShapeDtypeStruct((s, d), F32),
        scratch_shapes=[
            pltpu.VMEM((gpt, n_blocks, S5_BLOCK_W), BF16),
            pltpu.VMEM((gpt, n_blocks, S5_BLOCK_W), F32),
        ],
        compiler_params=_cparams(1),
        name="s5_core",
    )(u, d_row, toe, w_state, w_read, lam_r, lam_i)


def _s5_operators(lam_re, lam_im, log_step, b_re, b_im, c_re, c_im, n_blocks):
    g = lam_re.shape[0]
    hi = lax.Precision.HIGHEST
    dt = jnp.exp(log_step)[:, None]
    ar, ai = lam_re * dt, lam_im * dt

    def lam_powers(exponents):
        mag = jnp.exp(ar[:, None, :] * exponents[None, :, None])
        ang = ai[:, None, :] * exponents[None, :, None]
        return mag * jnp.cos(ang), mag * jnp.sin(ang)

    pr, pi = lam_powers(jnp.arange(S5_BLOCK + 1, dtype=F32))
    nr, ni = pr[:, 1] - 1.0, pi[:, 1]
    den = lam_re * lam_re + lam_im * lam_im
    qr = (nr * lam_re + ni * lam_im) / den
    qi = (ni * lam_re - nr * lam_im) / den
    bbr = b_re * qr[..., None] - b_im * qi[..., None]
    bbi = b_re * qi[..., None] + b_im * qr[..., None]
    cpr = c_re[:, None] * pr[:, :, None, :] - c_im[:, None] * pi[:, :, None, :]
    cpi = c_re[:, None] * pi[:, :, None, :] + c_im[:, None] * pr[:, :, None, :]
    kern = (jnp.einsum('gtcp,gpd->gtcd', cpr[:, :S5_BLOCK], bbr, precision=hi)
            - jnp.einsum('gtcp,gpd->gtcd', cpi[:, :S5_BLOCK], bbi, precision=hi))
    s_idx = jnp.arange(S5_BLOCK)
    lag_is = (s_idx[None, None, :] - s_idx[None, :, None]
              == s_idx[:, None, None]).astype(F32)
    toe = jnp.einsum('ust,gucd->gsdtc', lag_is, kern, precision=hi)
    toe = toe.reshape(g, S5_BLOCK_W, S5_BLOCK_W).astype(BF16)
    fr = jnp.flip(pr[:, :S5_BLOCK], axis=1)[:, :, None, :]
    fi = jnp.flip(pi[:, :S5_BLOCK], axis=1)[:, :, None, :]
    bt_r = jnp.swapaxes(bbr, 1, 2)[:, None]
    bt_i = jnp.swapaxes(bbi, 1, 2)[:, None]
    bst_r = (fr * bt_r - fi * bt_i).reshape(g, S5_BLOCK_W, S5_STATE)
    bst_i = (fr * bt_i + fi * bt_r).reshape(g, S5_BLOCK_W, S5_STATE)
    even = (jnp.arange(g) % 2 == 0)[:, None, None]
    zb = jnp.zeros_like(bst_r)
    w_state = jnp.where(even, jnp.concatenate([bst_r, zb, bst_i, zb], axis=-1),
                        jnp.concatenate([zb, bst_r, zb, bst_i], axis=-1)).astype(BF16)
    rd_r = cpr[:, 1:].transpose(0, 3, 1, 2).reshape(g, S5_STATE, S5_BLOCK_W)
    rd_i = -cpi[:, 1:].transpose(0, 3, 1, 2).reshape(g, S5_STATE, S5_BLOCK_W)
    zc = jnp.zeros_like(rd_r)
    w_read = jnp.where(even, jnp.concatenate([rd_r, zc, rd_i, zc], axis=1),
                       jnp.concatenate([zc, rd_r, zc, rd_i], axis=1)).astype(BF16)
    levels = int(math.log2(n_blocks))
    sr, si = lam_powers(S5_BLOCK * (2.0 ** jnp.arange(levels, dtype=F32)))
    sr = sr.reshape(g // 2, 2, levels, S5_STATE)
    si = si.reshape(g // 2, 2, levels, S5_STATE)
    lam_r = jnp.concatenate([sr[:, 0], sr[:, 1]], axis=-1)
    lam_i = jnp.concatenate([si[:, 0], si[:, 1]], axis=-1)
    return toe, w_state, w_read, lam_r, lam_i


def _s5_out_kernel(h_ref, a_ref, wa_ref, wb_ref, o_ref, a_scr):
    @pl.when(pl.program_id(1) == 0)
    def _():
        a_scr[...] = a_ref[...].astype(BF16)

    tm = a_scr.shape[0]
    rb = _tile(tm, EPILOGUE_ROW_SUBBLOCK)
    for r in range(tm // rb):
        rows = slice(r * rb, (r + 1) * rb)
        a = a_scr[rows, :]
        o_ref[rows, :] = (h_ref[rows, :]
                          + _dot(a, wa_ref[...]) * _sigmoid(_dot(a, wb_ref[...])))


def _s5_out(h, a, w_out, layer, *, tm, tn):
    s, d = h.shape
    nb = d // tn
    return pl.pallas_call(
        _s5_out_kernel,
        grid=(s // tm, nb),
        in_specs=[
            pl.BlockSpec((tm, tn), lambda i, j: (i, j)),
            pl.BlockSpec((tm, d), lambda i, j: (i, 0)),
            pl.BlockSpec((None, d, tn), lambda i, j: (layer, 0, j)),
            pl.BlockSpec((None, d, tn), lambda i, j: (layer, 0, nb + j)),
        ],
        out_specs=pl.BlockSpec((tm, tn), lambda i, j: (i, j)),
        out_shape=jax.ShapeDtypeStruct((s, d), F32),
        scratch_shapes=[pltpu.VMEM((tm, d), BF16)],
        compiler_params=_cparams(2),
        name="s5_out",
    )(h, a, w_out, w_out)


def _mlp_kernel(h_ref, g_ref, wu_ref, wd_ref, o_ref, hn_scr):
    @pl.when(pl.program_id(1) == 0)
    def _():
        h = h_ref[...]
        hn_scr[...] = _rms_rows(h, g_ref[...]).astype(BF16)
        o_ref[...] = h

    a = jnp.maximum(_dot(hn_scr[...], wu_ref[...]), 0.0)
    o_ref[...] += _dot((a * a).astype(BF16), wd_ref[...])


def _mlp(h, gain, w_up, w_down, layer, *, tm, tf):
    s, d = h.shape
    ff = w_up.shape[-1]
    return pl.pallas_call(
        _mlp_kernel,
        grid=(s // tm, ff // tf),
        in_specs=[
            pl.BlockSpec((tm, d), lambda i, k: (i, 0)),
            pl.BlockSpec((1, d), lambda i, k: (0, 0)),
            pl.BlockSpec((None, d, tf), lambda i, k: (layer, 0, k)),
            pl.BlockSpec((None, tf, d), lambda i, k: (layer, k, 0)),
        ],
        out_specs=pl.BlockSpec((tm, d), lambda i, k: (i, 0)),
        out_shape=jax.ShapeDtypeStruct((s, d), F32),
        scratch_shapes=[pltpu.VMEM((tm, d), BF16)],
        compiler_params=_cparams(2),
        name="mlp",
    )(h, gain, w_up, w_down)


def _ple_kernel(h_ref, g_ref, p_ref, wg_ref, wp_ref, o_ref, hn_scr, pb_scr, *, tn):
    j = pl.program_id(1)

    @pl.when(j == 0)
    def _():
        hn_scr[...] = _rms_rows(h_ref[...], g_ref[...]).astype(BF16)
        pb_scr[...] = p_ref[...].astype(BF16)

    cols = pl.ds(pl.multiple_of(j * tn, tn), tn)
    tm = hn_scr.shape[0]
    rb = _tile(tm, EPILOGUE_ROW_SUBBLOCK)
    for r in range(tm // rb):
        rows = slice(r * rb, (r + 1) * rb)
        gate = _sigmoid(_dot(hn_scr[rows, :], wg_ref[...]))
        emb = _dot(pb_scr[rows, :], wp_ref[...])
        o_ref[rows, :] = h_ref[rows, cols] + gate * emb


def _ple(h, gain, p, layer, batch, w_gate, w_proj, *, tm, tn):
    s, d = h.shape
    pd = p.shape[-1]
    return pl.pallas_call(
        functools.partial(_ple_kernel, tn=tn),
        grid=(s // tm, d // tn),
        in_specs=[
            pl.BlockSpec((tm, d), lambda i, j: (i, 0)),
            pl.BlockSpec((1, d), lambda i, j: (0, 0)),
            pl.BlockSpec((None, None, tm, pd), lambda i, j: (layer, batch, i, 0)),
            pl.BlockSpec((None, d, tn), lambda i, j: (layer, 0, j)),
            pl.BlockSpec((None, pd, tn), lambda i, j: (layer, 0, j)),
        ],
        out_specs=pl.BlockSpec((tm, tn), lambda i, j: (i, j)),
        out_shape=jax.ShapeDtypeStruct((s, d), F32),
        scratch_shapes=[pltpu.VMEM((tm, d), BF16), pltpu.VMEM((tm, pd), BF16)],
        compiler_params=_cparams(2),
        name="ple",
    )(h, gain, p, w_gate, w_proj)


def _final_norm_kernel(h_ref, g_ref, o_ref):
    o_ref[...] = _rms_rows(h_ref[...], g_ref[...])


def _final_norm(h, gain, *, tm):
    s, d = h.shape
    return pl.pallas_call(
        _final_norm_kernel,
        grid=(s // tm,),
        in_specs=[pl.BlockSpec((tm, d), lambda i: (i, 0)),
                  pl.BlockSpec((1, d), lambda i: (0, 0))],
        out_specs=pl.BlockSpec((tm, d), lambda i: (i, 0)),
        out_shape=jax.ShapeDtypeStruct((s, d), F32),
        compiler_params=_cparams(1),
        name="final_norm",
    )(h, gain)


def _tile(n, want):
    t = min(n, want)
    while n % t:
        t //= 2
    return t


def _gdn_layer(h, gain, w_in, w_in_f32, layer, conv_w, a_log, dt_bias, o_norm, w_out):
    s, d = h.shape
    tm = _tile(s, 1024)
    pad_v = GATE_LANES - 2 * GDN_V_HEADS
    w_ba = jnp.pad(w_in_f32[layer, :, GDN_CONV_DIM + GDN_VAL_DIM:].astype(BF16),
                   ((0, 0), (0, pad_v)))
    alog_row = jnp.concatenate([jnp.zeros((GDN_V_HEADS,), F32), a_log,
                                jnp.zeros((pad_v,), F32)])[None, :]
    dtb_row = jnp.concatenate([jnp.zeros((GDN_V_HEADS,), F32), dt_bias,
                               jnp.zeros((pad_v,), F32)])[None, :]
    qkv = _gdn_qkv(h, gain, w_in, layer, conv_w, tm=tm, tn=512)
    zs, gates_c = _gdn_zgate(h, gain, w_in, layer, w_ba, alog_row, dtb_row, tm=tm, tn=512)
    gates_r = gates_c.T
    on = _gdn_core(qkv, zs, gates_c, gates_r, o_norm[None, :], tb=_tile(s, 1024))
    return _res_matmul(h, on, w_out, layer, tm=tm, tn=512)


def _s5_layer(h, gain, w_in, layer, lam_re, lam_im, log_step, b_re, b_im, c_re, c_im, d_skip,
              w_out):
    s, d = h.shape
    u = _norm_matmul(h, gain, w_in, layer, tm=_tile(s, 1024), tn=512)
    operators = _s5_operators(lam_re, lam_im, log_step, b_re, b_im, c_re, c_im, s // S5_BLOCK)
    a = _s5_core(u, d_skip[None, :], *operators)
    return _s5_out(h, a, w_out, layer, tm=_tile(s, 1024), tn=512)


def kernel(x, p, norm_mix, norm_mlp, norm_ple, norm_final, gdn_w_in, gdn_conv_w, gdn_a_log,
           gdn_dt_bias, gdn_o_norm, gdn_w_out, s5_w_in, s5_lam_re, s5_lam_im, s5_log_step,
           s5_b_re, s5_b_im, s5_c_re, s5_c_im, s5_d, s5_w_out, mlp_w_up, mlp_w_down,
           ple_w_proj, ple_w_gate):
    bsz, s, d = x.shape
    depth = norm_mix.shape[0]
    gdn_w_in_b, gdn_w_out_b = gdn_w_in.astype(BF16), gdn_w_out.astype(BF16)
    s5_w_in_b, s5_w_out_b = s5_w_in.astype(BF16), s5_w_out.astype(BF16)
    mlp_w_up_b, mlp_w_down_b = mlp_w_up.astype(BF16), mlp_w_down.astype(BF16)
    ple_w_gate_b, ple_w_proj_b = ple_w_gate.astype(BF16), ple_w_proj.astype(BF16)
    outs = []
    for b in range(bsz):
        h = x[b]
        for i in range(depth):
            j = i // 2
            if i % 2 == 0:
                h = _gdn_layer(h, norm_mix[i][None, :], gdn_w_in_b, gdn_w_in, j, gdn_conv_w[j],
                               gdn_a_log[j], gdn_dt_bias[j], gdn_o_norm[j], gdn_w_out_b)
            else:
                h = _s5_layer(h, norm_mix[i][None, :], s5_w_in_b, j, s5_lam_re[j], s5_lam_im[j],
                              s5_log_step[j], s5_b_re[j], s5_b_im[j], s5_c_re[j], s5_c_im[j],
                              s5_d[j], s5_w_out_b)
            h = _mlp(h, norm_mlp[i][None, :], mlp_w_up_b, mlp_w_down_b, i,
                     tm=_tile(s, 1024), tf=512)
            h = _ple(h, norm_ple[i][None, :], p, i, b, ple_w_gate_b, ple_w_proj_b,
                     tm=_tile(s, 1024), tn=512)
        outs.append(_final_norm(h, norm_final[None, :], tm=_tile(s, 512)))
    return jnp.stack(outs, axis=0)
```

```python
import functools
import math

import jax
import jax.numpy as jnp
from jax import lax
from jax.experimental import pallas as pl
from jax.experimental.pallas import tpu as pltpu

F32 = jnp.float32
BF16 = jnp.bfloat16

NORM_EPS = 1e-6
L2_EPS = 1e-6

GDN_K_HEADS = 16
GDN_V_HEADS = 32
GDN_HEAD = 128
GDN_CONV = 4
GDN_KEY_DIM = GDN_K_HEADS * GDN_HEAD
GDN_VAL_DIM = GDN_V_HEADS * GDN_HEAD
GDN_CONV_DIM = 2 * GDN_KEY_DIM + GDN_VAL_DIM
GDN_CHUNK = 64
GDN_GROUP = 4 * GDN_CHUNK
EPILOGUE_ROW_SUBBLOCK = 256
GDN_STAGE_SKEW = 2
QKV_ROW_SUBBLOCK = 256
ZGATE_ROW_SUBBLOCK = 128
GATE_LANES = 128

S5_GROUP_CH = 16
S5_STATE = 64
S5_BLOCK = 16
S5_BLOCK_W = S5_BLOCK * S5_GROUP_CH

SUBLANES = 8
LANES = 128
S5_GROUPS_PER_TILE = LANES // S5_GROUP_CH
S5_ROW_CHUNK = 16
S5_RELAYOUT_UNROLL = 4
VMEM_LIMIT_BYTES = 56 * 1024 * 1024


def _cparams(n_axes):
    return pltpu.CompilerParams(
        dimension_semantics=("arbitrary",) * n_axes,
        vmem_limit_bytes=VMEM_LIMIT_BYTES)


def _rms_rows(x, gain):
    ms = jnp.mean(x * x, axis=-1, keepdims=True)
    return x * lax.rsqrt(ms + NORM_EPS) * gain


def _sigmoid(x):
    return 1.0 / (1.0 + jnp.exp(-x))


def _softplus(x):
    return jnp.maximum(x, 0.0) + jnp.log(1.0 + jnp.exp(-jnp.abs(x)))


def _gelu_tanh(x):
    c = math.sqrt(2.0 / math.pi)
    return 0.5 * x * (1.0 + jnp.tanh(c * (x + 0.044715 * (x * x * x))))


def _dot(a, b):
    return jnp.dot(a, b, preferred_element_type=F32)


def _gdn_qkv_kernel(h_ref, g_ref, w_ref, cw_ref, o_ref, hn_scr, tail_scr, buf_scr,
                    *, tm, tn, rb, n_qk_blocks):
    i = pl.program_id(0)
    j = pl.program_id(1)

    @pl.when(j == 0)
    def _():
        hn_scr[...] = _rms_rows(h_ref[...], g_ref[...]).astype(BF16)

    @pl.when(i == 0)
    def _():
        tail_scr[j] = jnp.zeros((SUBLANES, tn), F32)

    buf_scr[0:SUBLANES, :] = tail_scr[j]
    cw = cw_ref[...]

    def row_blocks(l2_norm):
        if l2_norm:
            scale = jnp.where(j < n_qk_blocks // 2, GDN_HEAD ** -0.5, 1.0).astype(F32)
        for r in range(tm // rb):
            rows = slice(r * rb, (r + 1) * rb)
            acc = _dot(hn_scr[rows, :], w_ref[...])
            buf_scr[SUBLANES + r * rb:SUBLANES + (r + 1) * rb, :] = acc
            y = acc * cw[3:4, :]
            for tap in range(GDN_CONV - 1):
                shift = GDN_CONV - 1 - tap
                y = y + buf_scr[pl.ds(SUBLANES + r * rb - shift, rb), :] * cw[tap:tap + 1, :]
            y = y * _sigmoid(y)
            if not l2_norm:
                o_ref[rows, :] = y
                continue
            for hh in range(tn // GDN_HEAD):
                sl = slice(hh * GDN_HEAD, (hh + 1) * GDN_HEAD)
                xs = y[:, sl]
                inv = lax.rsqrt(jnp.sum(xs * xs, axis=-1, keepdims=True) + L2_EPS)
                o_ref[rows, sl] = xs * (inv * scale)
        tail_scr[j] = buf_scr[tm:tm + SUBLANES, :]

    pl.when(j < n_qk_blocks)(functools.partial(row_blocks, True))
    pl.when(j >= n_qk_blocks)(functools.partial(row_blocks, False))


def _gdn_qkv(h, gain, w_in, layer, conv_w, *, tm, tn):
    s, d = h.shape
    n = conv_w.shape[1]
    kern = functools.partial(_gdn_qkv_kernel, tm=tm, tn=tn, rb=_tile(tm, QKV_ROW_SUBBLOCK),
                             n_qk_blocks=2 * GDN_KEY_DIM // tn)
    return pl.pallas_call(
        kern,
        grid=(s // tm, n // tn),
        in_specs=[
            pl.BlockSpec((tm, d), lambda i, j: (i, 0)),
            pl.BlockSpec((1, d), lambda i, j: (0, 0)),
            pl.BlockSpec((None, d, tn), lambda i, j: (layer, 0, j)),
            pl.BlockSpec((GDN_CONV, tn), lambda i, j: (0, j)),
        ],
        out_specs=pl.BlockSpec((tm, tn), lambda i, j: (i, j)),
        out_shape=jax.ShapeDtypeStruct((s, n), F32),
        scratch_shapes=[
            pltpu.VMEM((tm, d), BF16),
            pltpu.VMEM((n // tn, SUBLANES, tn), F32),
            pltpu.VMEM((tm + SUBLANES, tn), F32),
        ],
        compiler_params=_cparams(2),
        name="gdn_qkv",
    )(h, gain, w_in, conv_w)


def _gdn_zgate_kernel(h_ref, g_ref, wz_ref, wba_ref, alog_ref, dtb_ref, tri_ref,
                      z_ref, gate_ref, hn_scr, *, tm):
    j = pl.program_id(1)

    @pl.when(j == 0)
    def _():
        hn = _rms_rows(h_ref[...], g_ref[...]).astype(BF16)
        hn_scr[...] = hn
        ba = _dot(hn, wba_ref[...])
        beta = _sigmoid(ba)
        g = -jnp.exp(alog_ref[...]) * _softplus(ba + dtb_ref[...])
        lane = lax.broadcasted_iota(jnp.int32, (GDN_CHUNK, GATE_LANES), 1)
        tri = tri_ref[...]
        for c in range(tm // GDN_CHUNK):
            rows = slice(c * GDN_CHUNK, (c + 1) * GDN_CHUNK)
            gc = jnp.dot(tri, g[rows, :], preferred_element_type=F32,
                         precision=lax.Precision.HIGHEST)
            gate_ref[rows, :] = jnp.where(lane < GDN_V_HEADS, beta[rows, :], gc)

    rb = _tile(tm, ZGATE_ROW_SUBBLOCK)
    for r in range(tm // rb):
        rows = slice(r * rb, (r + 1) * rb)
        z = _dot(hn_scr[rows, :], wz_ref[...])
        z_ref[rows, :] = z * _sigmoid(z)


def _gdn_zgate(h, gain, w_in, layer, w_ba, alog_row, dtb_row, *, tm, tn):
    s, d = h.shape
    n = GDN_VAL_DIM
    z_off = GDN_CONV_DIM // tn
    tri = jnp.tril(jnp.ones((GDN_CHUNK, GDN_CHUNK), F32))
    return pl.pallas_call(
        functools.partial(_gdn_zgate_kernel, tm=tm),
        grid=(s // tm, n // tn),
        in_specs=[
            pl.BlockSpec((tm, d), lambda i, j: (i, 0)),
            pl.BlockSpec((1, d), lambda i, j: (0, 0)),
            pl.BlockSpec((None, d, tn), lambda i, j: (layer, 0, z_off + j)),
            pl.BlockSpec((d, GATE_LANES), lambda i, j: (0, 0)),
            pl.BlockSpec((1, GATE_LANES), lambda i, j: (0, 0)),
            pl.BlockSpec((1, GATE_LANES), lambda i, j: (0, 0)),
            pl.BlockSpec((GDN_CHUNK, GDN_CHUNK), lambda i, j: (0, 0)),
        ],
        out_specs=[
            pl.BlockSpec((tm, tn), lambda i, j: (i, j)),
            pl.BlockSpec((tm, GATE_LANES), lambda i, j: (i, 0)),
        ],
        out_shape=[
            jax.ShapeDtypeStruct((s, n), F32),
            jax.ShapeDtypeStruct((s, GATE_LANES), F32),
        ],
        scratch_shapes=[pltpu.VMEM((tm, d), BF16)],
        compiler_params=_cparams(2),
        name="gdn_zgate",
    )(h, gain, w_in, w_ba, alog_row, dtb_row, tri)


def _gdn_core_kernel(q_ref, k_ref, v_ref, z_ref, gc_ref, gr_ref, on_ref, o_ref,
                     s_scr, mq_scr, nn_scr, oi_scr, dl_scr, *, tb, n_t):
    kh = pl.program_id(0)
    t = pl.program_id(1)
    c_sz = GDN_CHUNK
    grp = GDN_GROUP
    n_in_grp = grp // c_sz
    n_chunks = tb // c_sz
    onorm = on_ref[...]

    @pl.when(t == 0)
    def _():
        s_scr[...] = jnp.zeros_like(s_scr)
        mq_scr[...] = jnp.zeros_like(mq_scr)
        nn_scr[...] = jnp.zeros_like(nn_scr)
        oi_scr[...] = jnp.zeros_like(oi_scr)
        dl_scr[...] = jnp.zeros_like(dl_scr)

    def recurrence_step(state, c):
        rows = slice(c * c_sz, (c + 1) * c_sz)
        for e in range(2):
            idx = e * n_chunks + c
            vcols = slice(e * GDN_HEAD, (e + 1) * GDN_HEAD)
            r = _dot(mq_scr[idx], state[e].astype(BF16))
            o = r[GDN_HEAD:] + oi_scr[idx]
            state[e] = state[e] * dl_scr[idx][0:1, :] - r[:GDN_HEAD] + nn_scr[idx]
            ms = jnp.mean(o * o, axis=-1, keepdims=True)
            o_ref[rows, vcols] = (o * lax.rsqrt(ms + NORM_EPS) * onorm
                                  * z_ref[rows, vcols]).astype(BF16)

    def drain_only():
        state = [s_scr[0], s_scr[1]]
        for c in range(n_chunks):
            recurrence_step(state, c)

    pl.when(t == n_t)(drain_only)
    pl.when(t < n_t)(functools.partial(
        _gdn_core_block, q_ref, k_ref, v_ref, gc_ref, gr_ref, s_scr, mq_scr, nn_scr, oi_scr,
        dl_scr, recurrence_step, kh, tb))


def _gdn_core_block(q_ref, k_ref, v_ref, gc_ref, gr_ref, s_scr, mq_scr, nn_scr, oi_scr,
                    dl_scr, recurrence_step, kh, tb):
    c_sz = GDN_CHUNK
    grp = GDN_GROUP
    n_in_grp = grp // c_sz
    n_chunks = tb // c_sz
    state = [s_scr[0], s_scr[1]]
    pending = list(range(n_chunks))

    def advance(count=1):
        for _ in range(min(count, len(pending))):
            recurrence_step(state, pending.pop(0))

    gates = gc_ref[...]
    lane = lax.broadcasted_iota(jnp.int32, (tb, GATE_LANES), 1)
    ri = lax.broadcasted_iota(jnp.int32, (grp, grp), 0)
    ci = lax.broadcasted_iota(jnp.int32, (grp, grp), 1)
    same_chunk = (ri // c_sz) == (ci // c_sz)
    bd_causal = same_chunk & (ri >= ci)
    bd_strict = same_chunk & (ri > ci)
    bd_mask = jnp.where(same_chunk, 1.0, 0.0).astype(BF16)
    pi = lax.broadcasted_iota(jnp.int32, (c_sz, grp), 0)
    pj = lax.broadcasted_iota(jnp.int32, (c_sz, grp), 1)
    pc = pj % c_sz
    eye_p = jnp.where(pi == pc, 1.0, 0.0).astype(F32)
    merge_sizes = [1 << lv for lv in range(int(math.log2(c_sz)))]
    merge_mask = {b: ((pi // (2 * b)) == (pc // (2 * b))) & ((pi // b) != (pc // b))
                  for b in merge_sizes}
    nt_dims = (((1,), (1,)), ((), ()))
    tn_dims = (((0,), (0,)), ((), ()))

    def block_diag(packed):
        return jnp.concatenate([packed] * n_in_grp, axis=0) * bd_mask

    per_head = []
    for e in range(2):
        head = 2 * kh + e
        beta_all = jnp.sum(jnp.where(lane == head, gates, 0.0), axis=-1, keepdims=True)
        gc_all = jnp.sum(jnp.where(lane == GDN_V_HEADS + head, gates, 0.0),
                         axis=-1, keepdims=True)
        gr_all = gr_ref[pl.ds(GDN_V_HEADS + head, 1), :]
        per_head.append((beta_all, gc_all, gr_all))

    new_ops = []

    def group_stages(g):
        rows = slice(g * grp, (g + 1) * grp)
        st = {}

        def gram():
            kg = k_ref[rows, :]
            qg = q_ref[rows, :]
            kb = kg.astype(BF16)
            st.update(kg=kg, qg=qg,
                      gkk=lax.dot_general(kb, kb, nt_dims, preferred_element_type=F32),
                      gqk=lax.dot_general(qg.astype(BF16), kb, nt_dims,
                                          preferred_element_type=F32))

        def setup():
            for e in range(2):
                beta_all, gc_all, gr_all = per_head[e]
                bcol = beta_all[rows, :]
                gcol = gc_all[rows, :]
                decay = jnp.exp(jnp.minimum(gcol - gr_all[:, rows], 0.0))
                a_nat = jnp.where(bd_strict, bcol * st["gkk"] * decay, 0.0)
                a_p = (a_nat[0:c_sz] + a_nat[c_sz:2 * c_sz]
                       + a_nat[2 * c_sz:3 * c_sz] + a_nat[3 * c_sz:4 * c_sz])
                st[e] = dict(bcol=bcol, gcol=gcol, decay=decay, a_p=a_p,
                             t_inv=eye_p - jnp.where(merge_mask[1], a_p, 0.0))

        def merge_left(b):
            for e in range(2):
                hd = st[e]
                join = jnp.where(merge_mask[b], hd["a_p"], 0.0).astype(BF16)
                hd["tl"] = _dot(hd["t_inv"].astype(BF16), block_diag(join)).astype(BF16)

        def merge_right(b):
            for e in range(2):
                hd = st[e]
                hd["t_inv"] = hd["t_inv"] - _dot(hd["tl"], block_diag(hd["t_inv"].astype(BF16)))

        def solve():
            for e in range(2):
                hd = st[e]
                vcols = slice(e * GDN_HEAD, (e + 1) * GDN_HEAD)
                eg = jnp.exp(hd["gcol"])
                rhs = jnp.concatenate([st["kg"] * (hd["bcol"] * eg),
                                       v_ref[rows, vcols] * hd["bcol"]], axis=1).astype(BF16)
                hd["eg"] = eg
                hd["wu"] = _dot(block_diag(hd["t_inv"].astype(BF16)), rhs).astype(BF16)

        def intra():
            for e in range(2):
                hd = st[e]
                qkm = jnp.where(bd_causal, st["gqk"] * hd["decay"], 0.0).astype(BF16)
                hd["qw"] = _dot(qkm, hd["wu"])

        def chunk_operators():
            for m in range(n_in_grp):
                cr = slice(m * c_sz, (m + 1) * c_sz)
                for e in range(2):
                    hd = st[e]
                    gcol = hd["gcol"]
                    glast = gcol[(m + 1) * c_sz - 1:(m + 1) * c_sz, :]
                    kd = (st["kg"][cr] * jnp.exp(glast - gcol[cr])).astype(BF16)
                    mn = lax.dot_general(kd, hd["wu"][cr], tn_dims,
                                         preferred_element_type=F32)
                    q_eff = (st["qg"][cr] * hd["eg"][cr]
                             - hd["qw"][cr, :GDN_HEAD]).astype(BF16)
                    new_ops.append((
                        e * n_chunks + g * n_in_grp + m,
                        jnp.concatenate([mn[:, :GDN_HEAD].astype(BF16), q_eff], axis=0),
                        mn[:, GDN_HEAD:],
                        hd["qw"][cr, GDN_HEAD:],
                        jnp.broadcast_to(jnp.exp(glast), (SUBLANES, GDN_HEAD))))

        merges = []
        for b in merge_sizes[1:]:
            merges += [functools.partial(merge_left, b), functools.partial(merge_right, b)]
        return [gram, setup] + merges + [solve, intra, chunk_operators]

    stage_lists = [group_stages(g) for g in range(tb // grp)]
    n_stages = len(stage_lists[0])
    n_slots = n_stages + GDN_STAGE_SKEW * (len(stage_lists) - 1)
    for slot in range(n_slots):
        for g, stages in enumerate(stage_lists):
            k = slot - GDN_STAGE_SKEW * g
            if 0 <= k < n_stages:
                stages[k]()
        advance(-(-n_chunks // n_slots))
    advance(n_chunks)
    s_scr[0] = state[0]
    s_scr[1] = state[1]
    for idx, mq_v, nn_v, oi_v, dl_v in new_ops:
        mq_scr[idx] = mq_v
        nn_scr[idx] = nn_v
        oi_scr[idx] = oi_v
        dl_scr[idx] = dl_v


def _gdn_core(qkv, zs, gates_c, gates_r, o_norm_row, *, tb):
    s = qkv.shape[0]
    nk = GDN_K_HEADS
    n_t = s // tb
    n_ops = 2 * (tb // GDN_CHUNK)

    def cur(kh, t):
        return jnp.minimum(t, n_t - 1)

    def prev(kh, t):
        return jnp.maximum(t - 1, 0)

    return pl.pallas_call(
        functools.partial(_gdn_core_kernel, tb=tb, n_t=n_t),
        grid=(nk, n_t + 1),
        in_specs=[
            pl.BlockSpec((tb, GDN_HEAD), lambda kh, t: (cur(kh, t), kh)),
            pl.BlockSpec((tb, GDN_HEAD), lambda kh, t: (cur(kh, t), nk + kh)),
            pl.BlockSpec((tb, 2 * GDN_HEAD), lambda kh, t: (cur(kh, t), nk + kh)),
            pl.BlockSpec((tb, 2 * GDN_HEAD), lambda kh, t: (prev(kh, t), kh)),
            pl.BlockSpec((tb, GATE_LANES), lambda kh, t: (cur(kh, t), 0)),
            pl.BlockSpec((GATE_LANES, tb), lambda kh, t: (0, cur(kh, t))),
            pl.BlockSpec((1, GDN_HEAD), lambda kh, t: (0, 0)),
        ],
        out_specs=pl.BlockSpec((tb, 2 * GDN_HEAD), lambda kh, t: (prev(kh, t), kh)),
        out_shape=jax.ShapeDtypeStruct((s, GDN_VAL_DIM), BF16),
        scratch_shapes=[
            pltpu.VMEM((2, GDN_HEAD, GDN_HEAD), F32),
            pltpu.VMEM((n_ops, GDN_HEAD + GDN_CHUNK, GDN_HEAD), BF16),
            pltpu.VMEM((n_ops, GDN_HEAD, GDN_HEAD), F32),
            pltpu.VMEM((n_ops, GDN_CHUNK, GDN_HEAD), F32),
            pltpu.VMEM((n_ops, SUBLANES, GDN_HEAD), F32),
        ],
        compiler_params=_cparams(2),
        name="gdn_core",
    )(qkv, qkv, qkv, zs, gates_c, gates_r, o_norm_row)


def _res_matmul_kernel(h_ref, a_ref, w_ref, o_ref):
    o_ref[...] = h_ref[...] + _dot(a_ref[...], w_ref[...])


def _res_matmul(h, a, w, layer, *, tm, tn):
    s, d = h.shape
    kdim = a.shape[1]
    return pl.pallas_call(
        _res_matmul_kernel,
        grid=(s // tm, d // tn),
        in_specs=[
            pl.BlockSpec((tm, tn), lambda i, j: (i, j)),
            pl.BlockSpec((tm, kdim), lambda i, j: (i, 0)),
            pl.BlockSpec((None, kdim, tn), lambda i, j: (layer, 0, j)),
        ],
        out_specs=pl.BlockSpec((tm, tn), lambda i, j: (i, j)),
        out_shape=jax.ShapeDtypeStruct((s, d), F32),
        compiler_params=_cparams(2),
        name="res_matmul",
    )(h, a, w)


def _norm_matmul_kernel(h_ref, g_ref, w_ref, o_ref, hn_scr):
    @pl.when(pl.program_id(1) == 0)
    def _():
        hn_scr[...] = _rms_rows(h_ref[...], g_ref[...]).astype(BF16)

    o_ref[...] = _dot(hn_scr[...], w_ref[...])


def _norm_matmul(h, gain, w, layer, *, tm, tn):
    s, d = h.shape
    n = w.shape[-1]
    return pl.pallas_call(
        _norm_matmul_kernel,
        grid=(s // tm, n // tn),
        in_specs=[
            pl.BlockSpec((tm, d), lambda i, j: (i, 0)),
            pl.BlockSpec((1, d), lambda i, j: (0, 0)),
            pl.BlockSpec((None, d, tn), lambda i, j: (layer, 0, j)),
        ],
        out_specs=pl.BlockSpec((tm, tn), lambda i, j: (i, j)),
        out_shape=jax.ShapeDtypeStruct((s, n), F32),
        scratch_shapes=[pltpu.VMEM((tm, d), BF16)],
        compiler_params=_cparams(2),
        name="s5_in_proj",
    )(h, gain, w)


def _s5_core_kernel(u_ref, d_ref, toe_ref, wst_ref, wrd_ref, lam_r_ref, lam_i_ref, a_ref,
                    uvec_scr, y_scr, *, n_blocks):
    gpt = S5_GROUPS_PER_TILE
    half = S5_BLOCK // 2
    rc = S5_ROW_CHUNK
    piece = lax.broadcasted_iota(jnp.int32, (rc, LANES), 1) // S5_GROUP_CH
    row = lax.broadcasted_iota(jnp.int32, (n_blocks, LANES), 0)

    def time_rows(n0, step):
        return pl.ds(n0 * S5_BLOCK + step, rc, stride=S5_BLOCK)

    def butterfly_transpose(vs, index, axis, unit):
        vs = list(vs)
        size = 8 * unit
        for dist in (4, 2, 1):
            low = (index // dist) % 2 == 0
            nxt = list(vs)
            for i in range(len(vs)):
                if (i // dist) % 2 == 0:
                    j = i + dist
                    nxt[i] = jnp.where(low, vs[i], pltpu.roll(vs[j], unit * dist, axis))
                    nxt[j] = jnp.where(low, pltpu.roll(vs[i], size - unit * dist, axis), vs[j])
            vs = nxt
        return vs

    def transpose_pieces(vs):
        return butterfly_transpose(vs, piece[:vs[0].shape[0]], 1, S5_GROUP_CH)

    def gather(it, carry):
        n0 = pl.multiple_of(it * rc, rc)
        halves = [transpose_pieces([u_ref[time_rows(n0, hh * half + sp), :]
                                    for sp in range(half)]) for hh in range(2)]
        for g in range(gpt):
            uvec_scr[g, pl.ds(n0, rc), :] = jnp.concatenate(
                [halves[0][g], halves[1][g]], axis=1).astype(BF16)
        return carry

    lax.fori_loop(0, n_blocks // rc, gather, 0, unroll=S5_RELAYOUT_UNROLL)

    y_intra = [_dot(uvec_scr[g], toe_ref[g]) for g in range(gpt)]
    res, ims = [], []
    for k in range(gpt // 2):
        st = (_dot(uvec_scr[2 * k], wst_ref[2 * k])
              + _dot(uvec_scr[2 * k + 1], wst_ref[2 * k + 1]))
        res.append(st[:, :LANES])
        ims.append(st[:, LANES:])
    for r in range(int(math.log2(n_blocks))):
        dist = 1 << r
        for k in range(gpt // 2):
            lr = lam_r_ref[k, r:r + 1, :]
            li = lam_i_ref[k, r:r + 1, :]
            sre = jnp.where(row >= dist, pltpu.roll(res[k], dist, 0), 0.0)
            sim = jnp.where(row >= dist, pltpu.roll(ims[k], dist, 0), 0.0)
            res[k] = res[k] + sre * lr - sim * li
            ims[k] = ims[k] + sre * li + sim * lr
    for k in range(gpt // 2):
        prev = jnp.concatenate([jnp.where(row >= 1, pltpu.roll(res[k], 1, 0), 0.0),
                                jnp.where(row >= 1, pltpu.roll(ims[k], 1, 0), 0.0)], axis=1)
        prev = prev.astype(BF16)
        y_scr[2 * k] = y_intra[2 * k] + _dot(prev, wrd_ref[2 * k])
        y_scr[2 * k + 1] = y_intra[2 * k + 1] + _dot(prev, wrd_ref[2 * k + 1])

    dsk = d_ref[...]

    sub = lax.broadcasted_iota(jnp.int32, (SUBLANES, LANES), 0)

    def scatter(it, carry):
        n0 = pl.multiple_of(it * SUBLANES, SUBLANES)
        for hh in range(2):
            cols = slice(hh * LANES, (hh + 1) * LANES)
            by_time = transpose_pieces([y_scr[g, pl.ds(n0, SUBLANES), cols] for g in range(gpt)])
            by_block = butterfly_transpose(by_time, sub, 0, 1)
            for i in range(SUBLANES):
                rows = pl.ds(pl.multiple_of((n0 + i) * S5_BLOCK + hh * half, half), half)
                a_ref[rows, :] = _gelu_tanh(by_block[i] + dsk * u_ref[rows, :])
        return carry

    lax.fori_loop(0, n_blocks // SUBLANES, scatter, 0, unroll=S5_RELAYOUT_UNROLL)


def _s5_core(u, d_row, toe, w_state, w_read, lam_r, lam_i):
    s, d = u.shape
    n_blocks = s // S5_BLOCK
    gpt = S5_GROUPS_PER_TILE
    levels = lam_r.shape[1]
    return pl.pallas_call(
        functools.partial(_s5_core_kernel, n_blocks=n_blocks),
        grid=(d // LANES,),
        in_specs=[
            pl.BlockSpec((s, LANES), lambda i: (0, i)),
            pl.BlockSpec((1, LANES), lambda i: (0, i)),
            pl.BlockSpec((gpt, S5_BLOCK_W, S5_BLOCK_W), lambda i: (i, 0, 0)),
            pl.BlockSpec((gpt, S5_BLOCK_W, 4 * S5_STATE), lambda i: (i, 0, 0)),
            pl.BlockSpec((gpt, 4 * S5_STATE, S5_BLOCK_W), lambda i: (i, 0, 0)),
            pl.BlockSpec((gpt // 2, levels, 2 * S5_STATE), lambda i: (i, 0, 0)),
            pl.BlockSpec((gpt // 2, levels, 2 * S5_STATE), lambda i: (i, 0, 0)),
        ],
        out_specs=pl.BlockSpec((s, LANES), lambda i: (0, i)),
        out_shape=jax.ShapeDtypeStruct((s, d), F32),
        scratch_shapes=[
            pltpu.VMEM((gpt, n_blocks, S5_BLOCK_W), BF16),
            pltpu.VMEM((gpt, n_blocks, S5_BLOCK_W), F32),
        ],
        compiler_params=_cparams(1),
        name="s5_core",
    )(u, d_row, toe, w_state, w_read, lam_r, lam_i)


def _s5_operators(lam_re, lam_im, log_step, b_re, b_im, c_re, c_im, n_blocks):
    g = lam_re.shape[0]
    hi = lax.Precision.HIGHEST
    dt = jnp.exp(log_step)[:, None]
    ar, ai = lam_re * dt, lam_im * dt

    def lam_powers(exponents):
        mag = jnp.exp(ar[:, None, :] * exponents[None, :, None])
        ang = ai[:, None, :] * exponents[None, :, None]
        return mag * jnp.cos(ang), mag * jnp.sin(ang)

    pr, pi = lam_powers(jnp.arange(S5_BLOCK + 1, dtype=F32))
    nr, ni = pr[:, 1] - 1.0, pi[:, 1]
    den = lam_re * lam_re + lam_im * lam_im
    qr = (nr * lam_re + ni * lam_im) / den
    qi = (ni * lam_re - nr * lam_im) / den
    bbr = b_re * qr[..., None] - b_im * qi[..., None]
    bbi = b_re * qi[..., None] + b_im * qr[..., None]
    cpr = c_re[:, None] * pr[:, :, None, :] - c_im[:, None] * pi[:, :, None, :]
    cpi = c_re[:, None] * pi[:, :, None, :] + c_im[:, None] * pr[:, :, None, :]
    kern = (jnp.einsum('gtcp,gpd->gtcd', cpr[:, :S5_BLOCK], bbr, precision=hi)
            - jnp.einsum('gtcp,gpd->gtcd', cpi[:, :S5_BLOCK], bbi, precision=hi))
    s_idx = jnp.arange(S5_BLOCK)
    lag_is = (s_idx[None, None, :] - s_idx[None, :, None]
              == s_idx[:, None, None]).astype(F32)
    toe = jnp.einsum('ust,gucd->gsdtc', lag_is, kern, precision=hi)
    toe = toe.reshape(g, S5_BLOCK_W, S5_BLOCK_W).astype(BF16)
    fr = jnp.flip(pr[:, :S5_BLOCK], axis=1)[:, :, None, :]
    fi = jnp.flip(pi[:, :S5_BLOCK], axis=1)[:, :, None, :]
    bt_r = jnp.swapaxes(bbr, 1, 2)[:, None]
    bt_i = jnp.swapaxes(bbi, 1, 2)[:, None]
    bst_r = (fr * bt_r - fi * bt_i).reshape(g, S5_BLOCK_W, S5_STATE)
    bst_i = (fr * bt_i + fi * bt_r).reshape(g, S5_BLOCK_W, S5_STATE)
    even = (jnp.arange(g) % 2 == 0)[:, None, None]
    zb = jnp.zeros_like(bst_r)
    w_state = jnp.where(even, jnp.concatenate([bst_r, zb, bst_i, zb], axis=-1),
                        jnp.concatenate([zb, bst_r, zb, bst_i], axis=-1)).astype(BF16)
    rd_r = cpr[:, 1:].transpose(0, 3, 1, 2).reshape(g, S5_STATE, S5_BLOCK_W)
    rd_i = -cpi[:, 1:].transpose(0, 3, 1, 2).reshape(g, S5_STATE, S5_BLOCK_W)
    zc = jnp.zeros_like(rd_r)
    w_read = jnp.where(even, jnp.concatenate([rd_r, zc, rd_i, zc], axis=1),
                       jnp.concatenate([zc, rd_r, zc, rd_i], axis=1)).astype(BF16)
    levels = int(math.log2(n_blocks))
    sr, si = lam_powers(S5_BLOCK * (2.0 ** jnp.arange(levels, dtype=F32)))
    sr = sr.reshape(g // 2, 2, levels, S5_STATE)
    si = si.reshape(g // 2, 2, levels, S5_STATE)
    lam_r = jnp.concatenate([sr[:, 0], sr[:, 1]], axis=-1)
    lam_i = jnp.concatenate([si[:, 0], si[:, 1]], axis=-1)
    return toe, w_state, w_read, lam_r, lam_i


def _s5_out_kernel(h_ref, a_ref, wa_ref, wb_ref, o_ref, a_scr):
    @pl.when(pl.program_id(1) == 0)
    def _():
        a_scr[...] = a_ref[...].astype(BF16)

    tm = a_scr.shape[0]
    rb = _tile(tm, EPILOGUE_ROW_SUBBLOCK)
    for r in range(tm // rb):
        rows = slice(r * rb, (r + 1) * rb)
        a = a_scr[rows, :]
        o_ref[rows, :] = (h_ref[rows, :]
                          + _dot(a, wa_ref[...]) * _sigmoid(_dot(a, wb_ref[...])))


def _s5_out(h, a, w_out, layer, *, tm, tn):
    s, d = h.shape
    nb = d // tn
    return pl.pallas_call(
        _s5_out_kernel,
        grid=(s // tm, nb),
        in_specs=[
            pl.BlockSpec((tm, tn), lambda i, j: (i, j)),
            pl.BlockSpec((tm, d), lambda i, j: (i, 0)),
            pl.BlockSpec((None, d, tn), lambda i, j: (layer, 0, j)),
            pl.BlockSpec((None, d, tn), lambda i, j: (layer, 0, nb + j)),
        ],
        out_specs=pl.BlockSpec((tm, tn), lambda i, j: (i, j)),
        out_shape=jax.ShapeDtypeStruct((s, d), F32),
        scratch_shapes=[pltpu.VMEM((tm, d), BF16)],
        compiler_params=_cparams(2),
        name="s5_out",
    )(h, a, w_out, w_out)


def _mlp_kernel(h_ref, g_ref, wu_ref, wd_ref, o_ref, hn_scr):
    @pl.when(pl.program_id(1) == 0)
    def _():
        h = h_ref[...]
        hn_scr[...] = _rms_rows(h, g_ref[...]).astype(BF16)
        o_ref[...] = h

    a = jnp.maximum(_dot(hn_scr[...], wu_ref[...]), 0.0)
    o_ref[...] += _dot((a * a).astype(BF16), wd_ref[...])


def _mlp(h, gain, w_up, w_down, layer, *, tm, tf):
    s, d = h.shape
    ff = w_up.shape[-1]
    return pl.pallas_call(
        _mlp_kernel,
        grid=(s // tm, ff // tf),
        in_specs=[
            pl.BlockSpec((tm, d), lambda i, k: (i, 0)),
            pl.BlockSpec((1, d), lambda i, k: (0, 0)),
            pl.BlockSpec((None, d, tf), lambda i, k: (layer, 0, k)),
            pl.BlockSpec((None, tf, d), lambda i, k: (layer, k, 0)),
        ],
        out_specs=pl.BlockSpec((tm, d), lambda i, k: (i, 0)),
        out_shape=jax.ShapeDtypeStruct((s, d), F32),
        scratch_shapes=[pltpu.VMEM((tm, d), BF16)],
        compiler_params=_cparams(2),
        name="mlp",
    )(h, gain, w_up, w_down)


def _ple_kernel(h_ref, g_ref, p_ref, wg_ref, wp_ref, gf_ref, o_ref, *, final_norm):
    tm = h_ref.shape[0]
    rb = _tile(tm, EPILOGUE_ROW_SUBBLOCK)
    for r in range(tm // rb):
        rows = slice(r * rb, (r + 1) * rb)
        h = h_ref[rows, :]
        hn = _rms_rows(h, g_ref[...]).astype(BF16)
        gate = _sigmoid(_dot(hn, wg_ref[...]))
        emb = _dot(p_ref[rows, :].astype(BF16), wp_ref[...])
        out = h + gate * emb
        if final_norm:
            out = _rms_rows(out, gf_ref[...])
        o_ref[rows, :] = out


def _ple(h, gain, p, layer, batch, w_gate, w_proj, final_gain, *, tm, final_norm):
    s, d = h.shape
    pd = p.shape[-1]
    return pl.pallas_call(
        functools.partial(_ple_kernel, final_norm=final_norm),
        grid=(s // tm,),
        in_specs=[
            pl.BlockSpec((tm, d), lambda i: (i, 0)),
            pl.BlockSpec((1, d), lambda i: (0, 0)),
            pl.BlockSpec((None, None, tm, pd), lambda i: (layer, batch, i, 0)),
            pl.BlockSpec((None, d, d), lambda i: (layer, 0, 0)),
            pl.BlockSpec((None, pd, d), lambda i: (layer, 0, 0)),
            pl.BlockSpec((1, d), lambda i: (0, 0)),
        ],
        out_specs=pl.BlockSpec((tm, d), lambda i: (i, 0)),
        out_shape=jax.ShapeDtypeStruct((s, d), F32),
        compiler_params=_cparams(1),
        name="ple",
    )(h, gain, p, w_gate, w_proj, final_gain)


def _tile(n, want):
    t = min(n, want)
    while n % t:
        t //= 2
    return t


def _gdn_layer(h, gain, w_in, w_in_f32, layer, conv_w, a_log, dt_bias, o_norm, w_out):
    s, d = h.shape
    tm = _tile(s, 1024)
    pad_v = GATE_LANES - 2 * GDN_V_HEADS
    w_ba = jnp.pad(w_in_f32[layer, :, GDN_CONV_DIM + GDN_VAL_DIM:].astype(BF16),
                   ((0, 0), (0, pad_v)))
    alog_row = jnp.concatenate([jnp.zeros((GDN_V_HEADS,), F32), a_log,
                                jnp.zeros((pad_v,), F32)])[None, :]
    dtb_row = jnp.concatenate([jnp.zeros((GDN_V_HEADS,), F32), dt_bias,
                               jnp.zeros((pad_v,), F32)])[None, :]
    qkv = _gdn_qkv(h, gain, w_in, layer, conv_w, tm=tm, tn=512)
    zs, gates_c = _gdn_zgate(h, gain, w_in, layer, w_ba, alog_row, dtb_row, tm=tm, tn=512)
    gates_r = gates_c.T
    on = _gdn_core(qkv, zs, gates_c, gates_r, o_norm[None, :], tb=_tile(s, 1024))
    return _res_matmul(h, on, w_out, layer, tm=tm, tn=512)


def _s5_layer(h, gain, w_in, layer, lam_re, lam_im, log_step, b_re, b_im, c_re, c_im, d_skip,
              w_out):
    s, d = h.shape
    u = _norm_matmul(h, gain, w_in, layer, tm=_tile(s, 1024), tn=512)
    operators = _s5_operators(lam_re, lam_im, log_step, b_re, b_im, c_re, c_im, s // S5_BLOCK)
    a = _s5_core(u, d_skip[None, :], *operators)
    return _s5_out(h, a, w_out, layer, tm=_tile(s, 1024), tn=512)


def kernel(x, p, norm_mix, norm_mlp, norm_ple, norm_final, gdn_w_in, gdn_conv_w, gdn_a_log,
           gdn_dt_bias, gdn_o_norm, gdn_w_out, s5_w_in, s5_lam_re, s5_lam_im, s5_log_step,
           s5_b_re, s5_b_im, s5_c_re, s5_c_im, s5_d, s5_w_out, mlp_w_up, mlp_w_down,
           ple_w_proj, ple_w_gate):
    bsz, s, d = x.shape
    depth = norm_mix.shape[0]
    gdn_w_in_b, gdn_w_out_b = gdn_w_in.astype(BF16), gdn_w_out.astype(BF16)
    s5_w_in_b, s5_w_out_b = s5_w_in.astype(BF16), s5_w_out.astype(BF16)
    mlp_w_up_b, mlp_w_down_b = mlp_w_up.astype(BF16), mlp_w_down.astype(BF16)
    ple_w_gate_b, ple_w_proj_b = ple_w_gate.astype(BF16), ple_w_proj.astype(BF16)
    outs = []
    for b in range(bsz):
        h = x[b]
        for i in range(depth):
            j = i // 2
            if i % 2 == 0:
                h = _gdn_layer(h, norm_mix[i][None, :], gdn_w_in_b, gdn_w_in, j, gdn_conv_w[j],
                               gdn_a_log[j], gdn_dt_bias[j], gdn_o_norm[j], gdn_w_out_b)
            else:
                h = _s5_layer(h, norm_mix[i][None, :], s5_w_in_b, j, s5_lam_re[j], s5_lam_im[j],
                              s5_log_step[j], s5_b_re[j], s5_b_im[j], s5_c_re[j], s5_c_im[j],
                              s5_d[j], s5_w_out_b)
            h = _mlp(h, norm_mlp[i][None, :], mlp_w_up_b, mlp_w_down_b, i,
                     tm=_tile(s, 1024), tf=512)
            h = _ple(h, norm_ple[i][None, :], p, i, b, ple_w_gate_b, ple_w_proj_b,
                     norm_final[None, :], tm=_tile(s, 512), final_norm=(i == depth - 1))
        outs.append(h)
    return jnp.stack(outs, axis=0)
```
